```python
import jax, jax.numpy as jnp
from jax import lax
import numpy as np

D_MODEL = 1024
BATCH = 8
SEQ = 2048
DEPTH = 4

N_MIXERS = 2
N_LRU = (DEPTH + N_MIXERS - 1) // N_MIXERS
N_SSD = DEPTH // N_MIXERS
CONV_W = 4
EPS = 1e-6
LRU_WIDTH = D_MODEL
LRU_HEADS = 8
LRU_BLOCK = LRU_WIDTH // LRU_HEADS
LRU_C = 8.0
SSD_D_INNER = 2 * D_MODEL
SSD_HEAD_DIM = 64
SSD_HEADS = SSD_D_INNER // SSD_HEAD_DIM
SSD_GROUPS = 4
SSD_HPG = SSD_HEADS // SSD_GROUPS
SSD_STATE = 128
SSD_CHUNK = 128
SSD_CONV_DIM = SSD_D_INNER + 2 * SSD_GROUPS * SSD_STATE
SSD_IN_DIM = SSD_D_INNER + SSD_CONV_DIM + SSD_HEADS
N_EXPERTS = 16
N_EXPERT_GROUPS = 4
EXPERTS_PER_GROUP = N_EXPERTS // N_EXPERT_GROUPS
TOP_K = 2
D_EXPERT = D_MODEL // 4

kernel_name = "hybrid_rglru_ssd_grouped_moe_adaln"


def rmsnorm(x, g):
    xf = x.astype(jnp.float32)
    y = xf * lax.rsqrt(jnp.mean(xf * xf, axis=-1, keepdims=True) + EPS)
    return (y * g.astype(jnp.float32)).astype(x.dtype)


def modulate(h, shift, scale):
    return h * (1.0 + scale[:, None, :]) + shift[:, None, :]


def causal_dwconv(x, w, b):
    k, ch = w.shape
    y = lax.conv_general_dilated(x, w[:, None, :].astype(x.dtype), window_strides=(1,),
                                 padding=[(k - 1, 0)], dimension_numbers=('NWC', 'WIO', 'NWC'),
                                 feature_group_count=ch)
    return y + b.astype(y.dtype)


def _lin_combine(left, right):
    a_l, b_l = left
    a_r, b_r = right
    return a_l * a_r, a_r * b_l + b_r


def rglru_mixer(h, w_in, conv_w, conv_b, wa, ba, wx, bx, lam, w_out):
    bsz, s, _ = h.shape
    u = h @ w_in
    gate_br, x_br = u[..., :LRU_WIDTH], u[..., LRU_WIDTH:]
    x_br = causal_dwconv(x_br, conv_w, conv_b)
    xb = x_br.astype(jnp.float32).reshape(bsz, s, LRU_HEADS, LRU_BLOCK)
    r = jax.nn.sigmoid(jnp.einsum('bshi,hij->bshj', xb, wa.astype(jnp.float32)) + ba.astype(jnp.float32).reshape(LRU_HEADS, LRU_BLOCK))
    i = jax.nn.sigmoid(jnp.einsum('bshi,hij->bshj', xb, wx.astype(jnp.float32)) + bx.astype(jnp.float32).reshape(LRU_HEADS, LRU_BLOCK))
    log_a = -LRU_C * r * jax.nn.softplus(-lam.astype(jnp.float32)).reshape(LRU_HEADS, LRU_BLOCK)
    a = jnp.exp(log_a)
    b = jnp.sqrt(-jnp.expm1(2.0 * log_a)) * (i * xb)
    _, hs = lax.associative_scan(_lin_combine, (a, b), axis=1)
    y = hs.reshape(bsz, s, LRU_WIDTH).astype(h.dtype) * jax.nn.gelu(gate_br, approximate=True)
    return y @ w_out


def ssd_mixer(h, w_in, conv_w, conv_b, dt_bias, a_log, d_skip, norm_g, w_out):
    bsz, s, _ = h.shape
    f32 = jnp.float32
    nc, L, G, E, P, N = s // SSD_CHUNK, SSD_CHUNK, SSD_GROUPS, SSD_HPG, SSD_HEAD_DIM, SSD_STATE
    proj = h @ w_in
    z = proj[..., :SSD_D_INNER]
    xbc = proj[..., SSD_D_INNER:SSD_D_INNER + SSD_CONV_DIM]
    dt = proj[..., SSD_D_INNER + SSD_CONV_DIM:]
    xbc = jax.nn.silu(causal_dwconv(xbc, conv_w, conv_b))
    xs = xbc[..., :SSD_D_INNER].astype(f32)
    Bm = xbc[..., SSD_D_INNER:SSD_D_INNER + G * N].astype(f32)
    Cm = xbc[..., SSD_D_INNER + G * N:].astype(f32)
    dt = jax.nn.softplus(dt.astype(f32) + dt_bias.astype(f32))
    A = -jnp.exp(a_log.astype(f32))
    xh = xs.reshape(bsz, s, SSD_HEADS, P)
    X = (xh * dt[..., None]).reshape(bsz, nc, L, G, E, P)
    dA = (dt * A).reshape(bsz, nc, L, G, E)
    Bc = Bm.reshape(bsz, nc, L, G, N)
    Cc = Cm.reshape(bsz, nc, L, G, N)
    Acs = jnp.cumsum(dA, axis=2)
    Acs_t = jnp.moveaxis(Acs, 2, -1)
    diff = Acs_t[..., :, None] - Acs_t[..., None, :]
    causal = jnp.tril(jnp.ones((L, L), dtype=bool))
    Lmat = jnp.exp(jnp.where(causal, diff, -jnp.inf))
    CB = jnp.einsum('bclgn,bcsgn->bcgls', Cc, Bc)
    y_diag = jnp.einsum('bcgels,bcsgep->bclgep', CB[:, :, :, None] * Lmat, X)
    decay_states = jnp.exp(Acs[:, :, -1:] - Acs)
    states = jnp.einsum('bclgn,bclge,bclgep->bcgepn', Bc, decay_states, X)
    chunk_decay = jnp.exp(Acs[:, :, -1])

    def step(carry, inp):
        st, dec = inp
        return carry * dec[..., None, None] + st, carry

    init = jnp.zeros((bsz, G, E, P, N), f32)
    _, prev = lax.scan(step, init, (jnp.moveaxis(states, 1, 0), jnp.moveaxis(chunk_decay, 1, 0)))
    prev = jnp.moveaxis(prev, 0, 1)
    y_off = jnp.einsum('bclgn,bcgepn,bclge->bclgep', Cc, prev, jnp.exp(Acs))
    y = (y_diag + y_off).reshape(bsz, s, SSD_HEADS, P) + d_skip.astype(f32)[:, None] * xh
    y = y.reshape(bsz, s, SSD_D_INNER) * jax.nn.silu(z.astype(f32))
    yg = y.reshape(bsz, s, G, SSD_D_INNER // G)
    yg = yg * lax.rsqrt(jnp.mean(yg * yg, axis=-1, keepdims=True) + EPS)
    y = yg.reshape(bsz, s, SSD_D_INNER) * norm_g.astype(f32)
    return y.astype(h.dtype) @ w_out


def grouped_moe(h, router_w, router_b, w_gate, w_up, w_down):
    bsz, s, _ = h.shape
    logits = h.astype(jnp.float32) @ router_w.astype(jnp.float32) + router_b.astype(jnp.float32)
    probs = jax.nn.softmax(logits, axis=-1)
    pg = probs.reshape(bsz, s, N_EXPERT_GROUPS, EXPERTS_PER_GROUP)
    group_score = lax.top_k(pg, TOP_K)[0].sum(-1)
    best = jnp.argmax(group_score, axis=-1)
    in_group = (jnp.arange(N_EXPERTS) // EXPERTS_PER_GROUP)[None, None, :] == best[..., None]
    masked = jnp.where(in_group, probs, -jnp.inf)
    top_vals, top_idx = lax.top_k(masked, TOP_K)
    top_w = top_vals / jnp.sum(top_vals, axis=-1, keepdims=True)
    gates = jnp.sum(jax.nn.one_hot(top_idx, N_EXPERTS, dtype=jnp.float32) * top_w[..., None], axis=-2)
    g = jnp.einsum('bsd,edf->bsef', h, w_gate)
    u = jnp.einsum('bsd,edf->bsef', h, w_up)
    act = jax.nn.silu(g) * u * gates.astype(g.dtype)[..., None]
    out = jnp.einsum('bsef,efd->bsd', act, w_down)
    return out.astype(h.dtype)


def setup_inputs(seed: int = 0) -> dict:
    key = jax.random.key(seed)
    ks = jax.random.split(key, 40)
    f32 = jnp.float32
    nrm = lambda k, shape, sc: jax.random.normal(k, shape, f32) * sc
    D = D_MODEL
    a0 = jax.random.uniform(ks[12], (N_LRU, LRU_WIDTH), f32, 0.9, 0.999)
    dt0 = jnp.exp(jax.random.uniform(ks[20], (N_SSD, SSD_HEADS), f32, np.log(1e-3), np.log(1e-1)))
    return {
        'x': nrm(ks[0], (BATCH, SEQ, D), 1.0),
        'c': nrm(ks[1], (BATCH, D), 1.0),
        'ada_w': nrm(ks[2], (DEPTH, D, 6 * D), 0.5 * D ** -0.5),
        'ada_b': nrm(ks[3], (DEPTH, 6 * D), 0.02),
        'norm1_g': 1.0 + nrm(ks[4], (DEPTH, D), 0.05),
        'norm2_g': 1.0 + nrm(ks[5], (DEPTH, D), 0.05),
        'lru_w_in': nrm(ks[6], (N_LRU, D, 2 * LRU_WIDTH), D ** -0.5),
        'lru_conv_w': nrm(ks[7], (N_LRU, CONV_W, LRU_WIDTH), CONV_W ** -0.5),
        'lru_conv_b': nrm(ks[8], (N_LRU, LRU_WIDTH), 0.02),
        'lru_wa': nrm(ks[9], (N_LRU, LRU_HEADS, LRU_BLOCK, LRU_BLOCK), LRU_BLOCK ** -0.5),
        'lru_ba': nrm(ks[10], (N_LRU, LRU_WIDTH), 0.02),
        'lru_wx': nrm(ks[11], (N_LRU, LRU_HEADS, LRU_BLOCK, LRU_BLOCK), LRU_BLOCK ** -0.5),
        'lru_bx': nrm(ks[13], (N_LRU, LRU_WIDTH), 0.02),
        'lru_lambda': jnp.log(a0) - jnp.log1p(-a0),
        'lru_w_out': nrm(ks[14], (N_LRU, LRU_WIDTH, D), LRU_WIDTH ** -0.5),
        'ssd_w_in': nrm(ks[15], (N_SSD, D, SSD_IN_DIM), D ** -0.5),
        'ssd_conv_w': nrm(ks[16], (N_SSD, CONV_W, SSD_CONV_DIM), CONV_W ** -0.5),
        'ssd_conv_b': nrm(ks[17], (N_SSD, SSD_CONV_DIM), 0.02),
        'ssd_dt_bias': dt0 + jnp.log(-jnp.expm1(-dt0)),
        'ssd_a_log': jnp.log(jax.random.uniform(ks[18], (N_SSD, SSD_HEADS), f32, 1.0, 16.0)),
        'ssd_d': 1.0 + nrm(ks[19], (N_SSD, SSD_HEADS), 0.1),
        'ssd_norm_g': 1.0 + nrm(ks[21], (N_SSD, SSD_D_INNER), 0.05),
        'ssd_w_out': nrm(ks[22], (N_SSD, SSD_D_INNER, D), SSD_D_INNER ** -0.5),
        'router_w': nrm(ks[23], (D, N_EXPERTS), D ** -0.5),
        'router_b': nrm(ks[24], (N_EXPERTS,), 0.01),
        'moe_w_gate': nrm(ks[25], (DEPTH, N_EXPERTS, D, D_EXPERT), D ** -0.5),
        'moe_w_up': nrm(ks[26], (DEPTH, N_EXPERTS, D, D_EXPERT), D ** -0.5),
        'moe_w_down': nrm(ks[27], (DEPTH, N_EXPERTS, D_EXPERT, D), D_EXPERT ** -0.5),
        'final_norm_g': 1.0 + nrm(ks[28], (D,), 0.05),
    }


def reference(x, c, ada_w, ada_b, norm1_g, norm2_g,
              lru_w_in, lru_conv_w, lru_conv_b, lru_wa, lru_ba, lru_wx, lru_bx, lru_lambda, lru_w_out,
              ssd_w_in, ssd_conv_w, ssd_conv_b, ssd_dt_bias, ssd_a_log, ssd_d, ssd_norm_g, ssd_w_out,
              router_w, router_b, moe_w_gate, moe_w_up, moe_w_down, final_norm_g):
    cond = jax.nn.silu(c)
    for i in range(DEPTH):
        mod = cond @ ada_w[i] + ada_b[i]
        sh1, sc1, g1, sh2, sc2, g2 = jnp.split(mod, 6, axis=-1)
        hn = modulate(rmsnorm(x, norm1_g[i]), sh1, sc1)
        j = i // N_MIXERS
        if i % N_MIXERS == 0:
            y = rglru_mixer(hn, lru_w_in[j], lru_conv_w[j], lru_conv_b[j], lru_wa[j], lru_ba[j],
                            lru_wx[j], lru_bx[j], lru_lambda[j], lru_w_out[j])
        else:
            y = ssd_mixer(hn, ssd_w_in[j], ssd_conv_w[j], ssd_conv_b[j], ssd_dt_bias[j], ssd_a_log[j],
                          ssd_d[j], ssd_norm_g[j], ssd_w_out[j])
        x = x + g1[:, None, :] * y
        hn = modulate(rmsnorm(x, norm2_g[i]), sh2, sc2)
        x = x + g2[:, None, :] * grouped_moe(hn, router_w, router_b, moe_w_gate[i], moe_w_up[i], moe_w_down[i])
    return rmsnorm(x, final_norm_g)
```

```python
import functools

import jax
import jax.numpy as jnp
from jax import lax
from jax.experimental import pallas as pl
from jax.experimental.pallas import tpu as pltpu

F32 = jnp.float32
BF16 = jnp.bfloat16

EPS = 1e-6
LRU_C = 8.0
SSD_CHUNK = 128
SSD_HEAD_DIM = 64
SSD_GROUPS = 4
SSD_STATE = 128
N_EXPERT_GROUPS = 4
TOP_K = 2

V7X_LANES = 128
V7X_SUBLANES = 8
V7X_VMEM_LIMIT_BYTES = 56 * 1024 * 1024


def _params(semantics):
    return pltpu.CompilerParams(dimension_semantics=semantics, vmem_limit_bytes=V7X_VMEM_LIMIT_BYTES)


def _resident(shape):
    return pl.BlockSpec(shape, lambda *_: (0,) * len(shape), pipeline_mode=pl.Buffered(1))


def _norm_mod(x, g, shift, scale):
    y = x * lax.rsqrt(jnp.mean(x * x, axis=-1, keepdims=True) + EPS)
    return (y * g) * (1.0 + scale) + shift


def _softplus(z):
    return jnp.maximum(z, 0.0) + jnp.log1p(jnp.exp(-jnp.abs(z)))


def _silu(z):
    return z * jax.nn.sigmoid(z)


def _dot(a, b):
    return jnp.dot(a, b, preferred_element_type=F32)


def _dot_nt(a, b, precision=None):
    return lax.dot_general(a, b, (((1,), (1,)), ((), ())), precision=precision, preferred_element_type=F32)


def _dot_tn(a, b):
    return lax.dot_general(a, b, (((0,), (0,)), ((), ())), preferred_element_type=F32)


def _ada_body(c_ref, w_ref, b_ref, o_ref):
    cond = _silu(c_ref[...]).astype(BF16)
    o_ref[...] = _dot(cond, w_ref[...].astype(BF16)) + b_ref[...]


def _ada_mod(c, ada_w, ada_b, tn=1536):
    depth, d, n = ada_w.shape
    bsz = c.shape[0]
    return pl.pallas_call(
        _ada_body,
        out_shape=jax.ShapeDtypeStruct((depth, bsz, n), F32),
        grid=(depth, n // tn),
        in_specs=[
            pl.BlockSpec((bsz, d), lambda l, j: (0, 0)),
            pl.BlockSpec((None, d, tn), lambda l, j: (l, 0, j)),
            pl.BlockSpec((None, 1, tn), lambda l, j: (l, 0, j)),
        ],
        out_specs=pl.BlockSpec((None, bsz, tn), lambda l, j: (l, 0, j)),
        compiler_params=_params(("parallel", "parallel")),
        name="ada_mod",
    )(c, ada_w, ada_b.reshape(depth, 1, n))


def _lru_body(x_ref, mod_ref, ng_ref, win_ref, cw_ref, cb_ref, wg_ref, bg_ref, lam_ref, wout_ref,
              o_ref, gate_scr, xpad_scr, a_scr, b_scr, h_scr, *, row_chunk):
    nb, ts, d = x_ref.shape
    w = gate_scr.shape[1]
    rows = nb * ts
    nh = wg_ref.shape[0]
    hw = w // nh
    kc = cw_ref.shape[0]
    halo = (kc - 1) * nb

    @pl.when(pl.program_id(0) == 0)
    def _():
        xpad_scr[0:halo, :] = jnp.zeros((halo, w), F32)
        h_scr[...] = jnp.zeros_like(h_scr)

    x = x_ref[...]
    mod = mod_ref[...]
    hn = _norm_mod(x, ng_ref[...], mod[:, 0:1, :], mod[:, 1:2, :])
    u = _dot(hn.reshape(rows, d).astype(BF16), win_ref[...])
    gate_scr[...] = u[:, :w]
    xbr = jnp.swapaxes(u[:, w:].reshape(nb, ts, w), 0, 1).reshape(rows, w)
    xpad_scr[halo:halo + rows, :] = xbr

    cw = cw_ref[...]
    cb = cb_ref[...]
    sp = _softplus(-lam_ref[...])

    def chunk(ci, carry):
        r0 = pl.multiple_of(ci * row_chunk, row_chunk)
        xc = cb
        for k in range(kc):
            xc = xc + cw[k:k + 1, :] * xpad_scr[pl.ds(r0 + k * nb, row_chunk), :]
        for h in range(nh):
            cols = slice(h * hw, (h + 1) * hw)
            xh = xc[:, cols]
            ri = _dot(xh.astype(BF16), wg_ref[h]) + bg_ref[h]
            r = jax.nn.sigmoid(ri[:, :hw])
            i = jax.nn.sigmoid(ri[:, hw:])
            log_a = (-LRU_C) * r * sp[:, cols]
            t = jnp.tanh(log_a)
            a_scr[pl.ds(r0, row_chunk), cols] = jnp.exp(log_a)
            b_scr[pl.ds(r0, row_chunk), cols] = jnp.sqrt(-2.0 * t / (1.0 - t)) * (i * xh)
        return carry

    lax.fori_loop(0, rows // row_chunk, chunk, 0)
    xpad_scr[0:halo, :] = xpad_scr[rows:rows + halo, :]

    def step(ti, h):
        r0 = pl.multiple_of(ti * nb, nb)
        h = a_scr[pl.ds(r0, nb), :] * h + b_scr[pl.ds(r0, nb), :]
        a_scr[pl.ds(r0, nb), :] = h
        return h

    h_scr[...] = lax.fori_loop(0, ts, step, h_scr[...], unroll=8)

    hs = jnp.swapaxes(a_scr[...].reshape(ts, nb, w), 0, 1).reshape(rows, w)
    y = (hs * jax.nn.gelu(gate_scr[...], approximate=True)).astype(BF16)
    out = _dot(y, wout_ref[...]).reshape(nb, ts, d)
    o_ref[...] = x + mod[:, 2:3, :] * out


def _lru_layer(x, mod, ng, win, cw, cb, wg, bg, lam, wout, *, ts=64, row_chunk=128):
    bsz, s, d = x.shape
    w = wout.shape[0]
    kc = cw.shape[0]
    rows = bsz * ts
    return pl.pallas_call(
        functools.partial(_lru_body, row_chunk=row_chunk),
        out_shape=jax.ShapeDtypeStruct(x.shape, F32),
        grid=(s // ts,),
        in_specs=[
            pl.BlockSpec((bsz, ts, d), lambda j: (0, j, 0)),
            _resident(mod.shape), _resident(ng.shape), _resident(win.shape), _resident(cw.shape),
            _resident(cb.shape), _resident(wg.shape), _resident(bg.shape), _resident(lam.shape),
            _resident(wout.shape),
        ],
        out_specs=pl.BlockSpec((bsz, ts, d), lambda j: (0, j, 0)),
        scratch_shapes=[
            pltpu.VMEM((rows, w), F32),
            pltpu.VMEM(((kc - 1) * bsz + rows, w), F32),
            pltpu.VMEM((rows, w), F32),
            pltpu.VMEM((rows, w), F32),
            pltpu.VMEM((bsz, w), F32),
        ],
        compiler_params=_params(("arbitrary",)),
        name="lru_layer",
    )(x, mod, ng, win, cw, cb, wg, bg, lam, wout)


def _ssd_body(x_ref, mod_ref, ng_ref, wz_ref, wxbc_ref, wdt_ref, cw_ref, cb_ref, dtb_ref, alog_ref,
              dexp_ref, nrm_ref, expand_ref, wout_ref,
              o_ref, z_scr, xpad_scr, act_scr, dt_scr, y_scr, state_scr):
    rows, d = x_ref.shape
    din = z_scr.shape[1]
    cdim = act_scr.shape[1]
    kc = cw_ref.shape[0]
    halo = V7X_SUBLANES
    L, P, G, N = SSD_CHUNK, SSD_HEAD_DIM, SSD_GROUPS, SSD_STATE
    gw = din // G
    hpg = gw // P

    @pl.when(pl.program_id(1) == 0)
    def _():
        xpad_scr[0:halo, :] = jnp.zeros((halo, cdim), F32)
        state_scr[...] = jnp.zeros_like(state_scr)

    x = x_ref[...]
    mod = mod_ref[...]
    hn = _norm_mod(x, ng_ref[...], mod[0:1, :], mod[1:2, :]).astype(BF16)
    z_scr[...] = _dot(hn, wz_ref[...])
    xpad_scr[halo:halo + rows, :] = _dot(hn, wxbc_ref[...])
    dt_scr[...] = _dot(hn, wdt_ref[...])

    cw = cw_ref[...]
    cb = cb_ref[...]

    def conv_chunk(ci, carry):
        r0 = pl.multiple_of(ci * L, L)
        win = xpad_scr[pl.ds(r0, L + halo), :]
        acc = cb
        for k in range(kc):
            off = halo - (kc - 1) + k
            acc = acc + cw[k:k + 1, :] * win[off:off + L, :]
        act_scr[pl.ds(r0, L), :] = _silu(acc)
        return carry

    lax.fori_loop(0, rows // L, conv_chunk, 0)
    xpad_scr[0:halo, :] = xpad_scr[rows:rows + halo, :]

    neg_a = -jnp.exp(alog_ref[...])
    dtb = dtb_ref[...]
    dexp = dexp_ref[...]
    nrm = nrm_ref[...]
    expand = expand_ref[...]
    row_i = lax.broadcasted_iota(jnp.int32, (L, L), 0)
    col_i = lax.broadcasted_iota(jnp.int32, (L, L), 1)
    causal = row_i >= col_i
    tril = causal.astype(F32)
    lane_lo = lax.broadcasted_iota(jnp.int32, (L, 2 * P), 1) < P

    def ssd_chunk(ci, carry):
        r0 = pl.multiple_of(ci * L, L)
        dt = _softplus(dt_scr[pl.ds(r0, L), :] + dtb)
        da = dt * neg_a
        acs = jnp.dot(tril, da, precision=lax.Precision.HIGHEST, preferred_element_type=F32)
        acs_t = acs.T
        dt_t = dt.T
        last = acs[L - 1:L, :]
        e_decay = jnp.exp(acs)
        w_state = dt * jnp.exp(last - acs)
        ex = _dot(jnp.concatenate([e_decay, w_state], axis=0).astype(BF16), expand)
        e_decay_x = ex[:L]
        w_state_x = ex[L:]
        xs = act_scr[pl.ds(r0, L), 0:din]
        xs_b = xs.astype(BF16)
        xd_b = (xs * w_state_x).astype(BF16)
        ys = []
        for g in range(G):
            bg_ = act_scr[pl.ds(r0, L), din + g * N: din + (g + 1) * N].astype(BF16)
            cg_ = act_scr[pl.ds(r0, L), din + (G + g) * N: din + (G + g + 1) * N].astype(BF16)
            cb_ = _dot_nt(cg_, bg_)
            gcols = slice(g * gw, (g + 1) * gw)
            parts = []
            for hp in range(hpg // 2):
                ms = []
                for hh in range(2):
                    h = g * hpg + 2 * hp + hh
                    seg = acs[:, h:h + 1] - acs_t[h:h + 1, :]
                    lmat = jnp.where(causal, jnp.exp(seg), 0.0)
                    ms.append((cb_ * lmat * dt_t[h:h + 1, :]).astype(BF16))
                c0 = g * gw + hp * 2 * P
                xp = xs_b[:, c0:c0 + 2 * P]
                zero = jnp.zeros_like(xp)
                rhs = jnp.concatenate([jnp.where(lane_lo, xp, zero), jnp.where(lane_lo, zero, xp)], axis=0)
                parts.append(_dot(jnp.concatenate(ms, axis=1), rhs))
            y_diag = jnp.concatenate(parts, axis=1)
            st_prev = state_scr[:, gcols]
            y_off = _dot(cg_, st_prev.astype(BF16)) * e_decay_x[:, gcols]
            st_new = _dot_tn(bg_, xd_b[:, gcols])
            state_scr[:, gcols] = st_prev * e_decay_x[L - 1:L, gcols] + st_new
            yg = y_diag + y_off + dexp[:, gcols] * xs[:, gcols]
            yg = yg * _silu(z_scr[pl.ds(r0, L), gcols])
            yg = yg * lax.rsqrt(jnp.mean(yg * yg, axis=-1, keepdims=True) + EPS)
            ys.append((yg * nrm[:, gcols]).astype(BF16))
        y_scr[pl.ds(r0, L), :] = jnp.concatenate(ys, axis=1)
        return carry

    lax.fori_loop(0, rows // L, ssd_chunk, 0)
    out = _dot(y_scr[...], wout_ref[...])
    o_ref[...] = x + mod[2:3, :] * out


def _ssd_layer(x, mod, ng, wz, wxbc, wdt, cw, cb, dtb, alog, dexp, nrm, expand, wout, *, rows=512):
    bsz, s, d = x.shape
    din = wz.shape[1]
    cdim = wxbc.shape[1]
    res = [ng, wz, wxbc, wdt, cw, cb, dtb, alog, dexp, nrm, expand, wout]
    return pl.pallas_call(
        _ssd_body,
        out_shape=jax.ShapeDtypeStruct(x.shape, F32),
        grid=(bsz, s // rows),
        in_specs=[
            pl.BlockSpec((None, rows, d), lambda b, j: (b, j, 0)),
            pl.BlockSpec((None, 6, d), lambda b, j: (b, 0, 0)),
        ] + [_resident(a.shape) for a in res],
        out_specs=pl.BlockSpec((None, rows, d), lambda b, j: (b, j, 0)),
        scratch_shapes=[
            pltpu.VMEM((rows, din), F32),
            pltpu.VMEM((V7X_SUBLANES + rows, cdim), F32),
            pltpu.VMEM((rows, cdim), F32),
            pltpu.VMEM((rows, V7X_LANES), F32),
            pltpu.VMEM((rows, din), BF16),
            pltpu.VMEM((SSD_STATE, din), F32),
        ],
        compiler_params=_params(("parallel", "arbitrary")),
        name="ssd_layer",
    )(x, mod, *res)


def _route(logits_t, n_groups):
    ne, tm = logits_t.shape
    per = ne // n_groups
    ex = jnp.exp(logits_t - jnp.max(logits_t, axis=0, keepdims=True))
    probs = ex / jnp.sum(ex, axis=0, keepdims=True)
    p3 = probs.reshape(n_groups, per, tm)
    io = lax.broadcasted_iota(jnp.int32, p3.shape, 1)
    m1 = jnp.max(p3, axis=1, keepdims=True)
    sel1 = io == jnp.min(jnp.where(p3 == m1, io, per), axis=1, keepdims=True)
    rest = jnp.where(sel1, -1.0, p3)
    m2 = jnp.max(rest, axis=1, keepdims=True)
    sel2 = io == jnp.min(jnp.where(rest == m2, io, per), axis=1, keepdims=True)
    score = m1 + m2
    gio = lax.broadcasted_iota(jnp.int32, score.shape, 0)
    best = jnp.min(jnp.where(score == jnp.max(score, axis=0, keepdims=True), gio, n_groups),
                   axis=0, keepdims=True)
    top = jnp.where(sel1, m1, 0.0) + jnp.where(sel2, m2, 0.0)
    gates = jnp.where(gio == best, top / score, 0.0)
    return gates.reshape(ne, tm)


def _moe_body(x_ref, mod_ref, ng_ref, rwt_ref, rb_ref, wg_ref, wu_ref, wd_ref, fg_ref,
              o_ref, act_scr, *, final_norm):
    tm, d = x_ref.shape
    ne, _, de = wg_ref.shape
    x = x_ref[...]
    mod = mod_ref[...]
    hn = _norm_mod(x, ng_ref[...], mod[3:4, :], mod[4:5, :])
    hb = hn.astype(BF16)
    logits_t = _dot_nt(rwt_ref[...], hn, precision=lax.Precision.HIGHEST) + rb_ref[...]
    gates_t = _route(logits_t, N_EXPERT_GROUPS)
    pad = jnp.zeros((V7X_LANES - ne, tm), F32)
    gates = jnp.concatenate([gates_t, pad], axis=0).T
    for e in range(ne):
        g = _dot(hb, wg_ref[e])
        u = _dot(hb, wu_ref[e])
        act_scr[:, e * de:(e + 1) * de] = (_silu(g) * u * gates[:, e:e + 1]).astype(BF16)
    out = _dot(act_scr[...], wd_ref[...])
    y = x + mod[5:6, :] * out
    if final_norm:
        y = y * lax.rsqrt(jnp.mean(y * y, axis=-1, keepdims=True) + EPS) * fg_ref[...]
    o_ref[...] = y


def _moe_layer(x, mod, ng, rwt, rb, wg, wu, wd, fg, *, final_norm, tm=512):
    bsz, s, d = x.shape
    ne, _, de = wg.shape
    res = [ng, rwt, rb, wg, wu, wd, fg]
    return pl.pallas_call(
        functools.partial(_moe_body, final_norm=final_norm),
        out_shape=jax.ShapeDtypeStruct(x.shape, F32),
        grid=(bsz, s // tm),
        in_specs=[
            pl.BlockSpec((None, tm, d), lambda b, j: (b, j, 0)),
            pl.BlockSpec((None, 6, d), lambda b, j: (b, 0, 0)),
        ] + [_resident(a.shape) for a in res],
        out_specs=pl.BlockSpec((None, tm, d), lambda b, j: (b, j, 0)),
        scratch_shapes=[pltpu.VMEM((tm, ne * de), BF16)],
        compiler_params=_params(("parallel", "parallel")),
        name="moe_layer",
    )(x, mod, *res)


def kernel(x, c, ada_w, ada_b, norm1_g, norm2_g, lru_w_in, lru_conv_w, lru_conv_b, lru_wa, lru_ba, lru_wx, lru_bx, lru_lambda, lru_w_out, ssd_w_in, ssd_conv_w, ssd_conv_b, ssd_dt_bias, ssd_a_log, ssd_d, ssd_norm_g, ssd_w_out, router_w, router_b, moe_w_gate, moe_w_up, moe_w_down, final_norm_g):
    bsz, s, d = x.shape
    depth = ada_w.shape[0]
    n_mixers = 2
    mod = _ada_mod(c, ada_w, ada_b).reshape(depth, bsz, 6, d)

    heads = ssd_dt_bias.shape[1]
    din = heads * SSD_HEAD_DIM
    cdim = din + 2 * SSD_GROUPS * SSD_STATE
    lane_pad = V7X_LANES - heads
    expand = (jnp.arange(V7X_LANES)[:, None] == (jnp.arange(din) // SSD_HEAD_DIM)[None, :]).astype(BF16)

    ne = router_w.shape[1]
    rwt = router_w.T
    rb = router_b.reshape(ne, 1)
    fg = final_norm_g.reshape(1, d)

    for i in range(depth):
        j = i // n_mixers
        ng1 = norm1_g[i].reshape(1, d)
        if i % n_mixers == 0:
            nh = lru_wa.shape[1]
            w = lru_w_out.shape[1]
            wg = jnp.concatenate([lru_wa[j], lru_wx[j]], axis=-1).astype(BF16)
            bg = jnp.concatenate([lru_ba[j].reshape(nh, 1, w // nh), lru_bx[j].reshape(nh, 1, w // nh)], axis=-1)
            x = _lru_layer(x, mod[i], ng1, lru_w_in[j].astype(BF16), lru_conv_w[j], lru_conv_b[j].reshape(1, w),
                           wg, bg, lru_lambda[j].reshape(1, w), lru_w_out[j].astype(BF16))
        else:
            w_in = ssd_w_in[j]
            wz = w_in[:, :din].astype(BF16)
            wxbc = w_in[:, din:din + cdim].astype(BF16)
            wdt = jnp.pad(w_in[:, din + cdim:], ((0, 0), (0, lane_pad))).astype(BF16)
            dtb = jnp.pad(ssd_dt_bias[j], (0, lane_pad)).reshape(1, V7X_LANES)
            alog = jnp.pad(ssd_a_log[j], (0, lane_pad)).reshape(1, V7X_LANES)
            dexp = jnp.repeat(ssd_d[j], SSD_HEAD_DIM).reshape(1, din)
            x = _ssd_layer(x, mod[i], ng1, wz, wxbc, wdt, ssd_conv_w[j], ssd_conv_b[j].reshape(1, cdim),
                           dtb, alog, dexp, ssd_norm_g[j].reshape(1, din), expand, ssd_w_out[j].astype(BF16))
        de = moe_w_gate.shape[-1]
        x = _moe_layer(x, mod[i], norm2_g[i].reshape(1, d), rwt, rb,
                       moe_w_gate[i].astype(BF16), moe_w_up[i].astype(BF16),
                       moe_w_down[i].reshape(ne * de, d).astype(BF16), fg,
                       final_norm=(i == depth - 1))
    return x
```

```python
import functools

import jax
import jax.numpy as jnp
from jax import lax
from jax.experimental import pallas as pl
from jax.experimental.pallas import tpu as pltpu

F32 = jnp.float32
BF16 = jnp.bfloat16

EPS = 1e-6
LRU_C = 8.0
SSD_CHUNK = 128
SSD_HEAD_DIM = 64
SSD_GROUPS = 4
SSD_STATE = 128
N_EXPERT_GROUPS = 4
TOP_K = 2

V7X_LANES = 128
V7X_SUBLANES = 8
V7X_VMEM_LIMIT_BYTES = 56 * 1024 * 1024
V7X_BF16_ROWS = 16

MOE_SUB = 256
ROW_ALIGN = V7X_BF16_ROWS
MOE_CHUNK = 256


def _params(semantics):
    return pltpu.CompilerParams(dimension_semantics=semantics, vmem_limit_bytes=V7X_VMEM_LIMIT_BYTES)


def _resident(shape):
    return pl.BlockSpec(shape, lambda *_: (0,) * len(shape), pipeline_mode=pl.Buffered(1))


def _norm_mod(x, g, shift, scale):
    y = x * lax.rsqrt(jnp.mean(x * x, axis=-1, keepdims=True) + EPS)
    return (y * g) * (1.0 + scale) + shift


def _softplus(z):
    return jnp.maximum(z, 0.0) + jnp.log1p(jnp.exp(-jnp.abs(z)))


def _silu(z):
    return z * jax.nn.sigmoid(z)


def _dot(a, b):
    return jnp.dot(a, b, preferred_element_type=F32)


def _dot_nt(a, b, precision=None):
    return lax.dot_general(a, b, (((1,), (1,)), ((), ())), precision=precision, preferred_element_type=F32)


def _dot_tn(a, b):
    return lax.dot_general(a, b, (((0,), (0,)), ((), ())), preferred_element_type=F32)


def _ada_body(c_ref, w_ref, b_ref, o_ref):
    cond = _silu(c_ref[...]).astype(BF16)
    o_ref[...] = _dot(cond, w_ref[...].astype(BF16)) + b_ref[...]


def _ada_mod(c, ada_w, ada_b, tn=1536):
    depth, d, n = ada_w.shape
    bsz = c.shape[0]
    return pl.pallas_call(
        _ada_body,
        out_shape=jax.ShapeDtypeStruct((depth, bsz, n), F32),
        grid=(depth, n // tn),
        in_specs=[
            pl.BlockSpec((bsz, d), lambda l, j: (0, 0)),
            pl.BlockSpec((None, d, tn), lambda l, j: (l, 0, j)),
            pl.BlockSpec((None, 1, tn), lambda l, j: (l, 0, j)),
        ],
        out_specs=pl.BlockSpec((None, bsz, tn), lambda l, j: (l, 0, j)),
        compiler_params=_params(("parallel", "parallel")),
        name="ada_mod",
    )(c, ada_w, ada_b.reshape(depth, 1, n))


def _lru_body(x_ref, mod_ref, ng_ref, win_ref, cw_ref, cb_ref, wg_ref, bg_ref, lam_ref, wout_ref,
              o_ref, gate_scr, xpad_scr, a_scr, b_scr, h_scr, *, row_chunk):
    nb, ts, d = x_ref.shape
    w = gate_scr.shape[1]
    rows = nb * ts
    nh = wg_ref.shape[0]
    hw = w // nh
    kc = cw_ref.shape[0]
    halo = (kc - 1) * nb

    @pl.when(pl.program_id(0) == 0)
    def _():
        xpad_scr[0:halo, :] = jnp.zeros((halo, w), F32)
        h_scr[...] = jnp.zeros_like(h_scr)

    x = x_ref[...]
    mod = mod_ref[...]
    hn = _norm_mod(x, ng_ref[...], mod[:, 0:1, :], mod[:, 1:2, :])
    u = _dot(hn.reshape(rows, d).astype(BF16), win_ref[...])
    gate_scr[...] = u[:, :w]
    xbr = jnp.swapaxes(u[:, w:].reshape(nb, ts, w), 0, 1).reshape(rows, w)
    xpad_scr[halo:halo + rows, :] = xbr

    cw = cw_ref[...]
    cb = cb_ref[...]
    sp = _softplus(-lam_ref[...])

    def chunk(ci, carry):
        r0 = pl.multiple_of(ci * row_chunk, row_chunk)
        xc = cb
        for k in range(kc):
            xc = xc + cw[k:k + 1, :] * xpad_scr[pl.ds(r0 + k * nb, row_chunk), :]
        for h in range(nh):
            cols = slice(h * hw, (h + 1) * hw)
            xh = xc[:, cols]
            ri = _dot(xh.astype(BF16), wg_ref[h]) + bg_ref[h]
            r = jax.nn.sigmoid(ri[:, :hw])
            i = jax.nn.sigmoid(ri[:, hw:])
            log_a = (-LRU_C) * r * sp[:, cols]
            t = jnp.tanh(log_a)
            a_scr[pl.ds(r0, row_chunk), cols] = jnp.exp(log_a)
            b_scr[pl.ds(r0, row_chunk), cols] = jnp.sqrt(-2.0 * t / (1.0 - t)) * (i * xh)
        return carry

    lax.fori_loop(0, rows // row_chunk, chunk, 0)
    xpad_scr[0:halo, :] = xpad_scr[rows:rows + halo, :]

    def step(ti, h):
        r0 = pl.multiple_of(ti * nb, nb)
        h = a_scr[pl.ds(r0, nb), :] * h + b_scr[pl.ds(r0, nb), :]
        a_scr[pl.ds(r0, nb), :] = h
        return h

    h_scr[...] = lax.fori_loop(0, ts, step, h_scr[...], unroll=8)

    hs = jnp.swapaxes(a_scr[...].reshape(ts, nb, w), 0, 1).reshape(rows, w)
    y = (hs * jax.nn.gelu(gate_scr[...], approximate=True)).astype(BF16)
    out = _dot(y, wout_ref[...]).reshape(nb, ts, d)
    o_ref[...] = x + mod[:, 2:3, :] * out


def _lru_layer(x, mod, ng, win, cw, cb, wg, bg, lam, wout, *, ts=64, row_chunk=128):
    bsz, s, d = x.shape
    w = wout.shape[0]
    kc = cw.shape[0]
    rows = bsz * ts
    return pl.pallas_call(
        functools.partial(_lru_body, row_chunk=row_chunk),
        out_shape=jax.ShapeDtypeStruct(x.shape, F32),
        grid=(s // ts,),
        in_specs=[
            pl.BlockSpec((bsz, ts, d), lambda j: (0, j, 0)),
            _resident(mod.shape), _resident(ng.shape), _resident(win.shape), _resident(cw.shape),
            _resident(cb.shape), _resident(wg.shape), _resident(bg.shape), _resident(lam.shape),
            _resident(wout.shape),
        ],
        out_specs=pl.BlockSpec((bsz, ts, d), lambda j: (0, j, 0)),
        scratch_shapes=[
            pltpu.VMEM((rows, w), F32),
            pltpu.VMEM(((kc - 1) * bsz + rows, w), F32),
            pltpu.VMEM((rows, w), F32),
            pltpu.VMEM((rows, w), F32),
            pltpu.VMEM((bsz, w), F32),
        ],
        compiler_params=_params(("arbitrary",)),
        name="lru_layer",
    )(x, mod, ng, win, cw, cb, wg, bg, lam, wout)


def _ssd_body(x_ref, mod_ref, ng_ref, wz_ref, wxbc_ref, wdt_ref, cw_ref, cb_ref, dtb_ref, alog_ref,
              dexp_ref, nrm_ref, expand_ref, wout_ref,
              o_ref, z_scr, xpad_scr, act_scr, dt_scr, y_scr, state_scr):
    rows, d = x_ref.shape
    din = z_scr.shape[1]
    cdim = act_scr.shape[1]
    kc = cw_ref.shape[0]
    halo = V7X_SUBLANES
    L, P, G, N = SSD_CHUNK, SSD_HEAD_DIM, SSD_GROUPS, SSD_STATE
    gw = din // G
    hpg = gw // P

    @pl.when(pl.program_id(1) == 0)
    def _():
        xpad_scr[0:halo, :] = jnp.zeros((halo, cdim), F32)
        state_scr[...] = jnp.zeros_like(state_scr)

    x = x_ref[...]
    mod = mod_ref[...]
    hn = _norm_mod(x, ng_ref[...], mod[0:1, :], mod[1:2, :]).astype(BF16)
    z_scr[...] = _dot(hn, wz_ref[...])
    xpad_scr[halo:halo + rows, :] = _dot(hn, wxbc_ref[...])
    dt_scr[...] = _dot(hn, wdt_ref[...])

    cw = cw_ref[...]
    cb = cb_ref[...]

    def conv_chunk(ci, carry):
        r0 = pl.multiple_of(ci * L, L)
        win = xpad_scr[pl.ds(r0, L + halo), :]
        acc = cb
        for k in range(kc):
            off = halo - (kc - 1) + k
            acc = acc + cw[k:k + 1, :] * win[off:off + L, :]
        act_scr[pl.ds(r0, L), :] = _silu(acc)
        return carry

    lax.fori_loop(0, rows // L, conv_chunk, 0)
    xpad_scr[0:halo, :] = xpad_scr[rows:rows + halo, :]

    neg_a = -jnp.exp(alog_ref[...])
    dtb = dtb_ref[...]
    dexp = dexp_ref[...]
    nrm = nrm_ref[...]
    expand = expand_ref[...]
    row_i = lax.broadcasted_iota(jnp.int32, (L, L), 0)
    col_i = lax.broadcasted_iota(jnp.int32, (L, L), 1)
    causal = row_i >= col_i
    tril = causal.astype(F32)
    lane_lo = lax.broadcasted_iota(jnp.int32, (L, 2 * P), 1) < P

    def ssd_chunk(ci, carry):
        r0 = pl.multiple_of(ci * L, L)
        dt = _softplus(dt_scr[pl.ds(r0, L), :] + dtb)
        da = dt * neg_a
        acs = jnp.dot(tril, da, precision=lax.Precision.HIGHEST, preferred_element_type=F32)
        acs_t = acs.T
        dt_t = dt.T
        last = acs[L - 1:L, :]
        e_decay = jnp.exp(acs)
        w_state = dt * jnp.exp(last - acs)
        ex = _dot(jnp.concatenate([e_decay, w_state], axis=0).astype(BF16), expand)
        e_decay_x = ex[:L]
        w_state_x = ex[L:]
        xs = act_scr[pl.ds(r0, L), 0:din]
        xs_b = xs.astype(BF16)
        xd_b = (xs * w_state_x).astype(BF16)
        ys = []
        for g in range(G):
            bg_ = act_scr[pl.ds(r0, L), din + g * N: din + (g + 1) * N].astype(BF16)
            cg_ = act_scr[pl.ds(r0, L), din + (G + g) * N: din + (G + g + 1) * N].astype(BF16)
            cb_ = _dot_nt(cg_, bg_)
            gcols = slice(g * gw, (g + 1) * gw)
            parts = []
            for hp in range(hpg // 2):
                ms = []
                for hh in range(2):
                    h = g * hpg + 2 * hp + hh
                    seg = acs[:, h:h + 1] - acs_t[h:h + 1, :]
                    lmat = jnp.where(causal, jnp.exp(seg), 0.0)
                    ms.append((cb_ * lmat * dt_t[h:h + 1, :]).astype(BF16))
                c0 = g * gw + hp * 2 * P
                xp = xs_b[:, c0:c0 + 2 * P]
                zero = jnp.zeros_like(xp)
                rhs = jnp.concatenate([jnp.where(lane_lo, xp, zero), jnp.where(lane_lo, zero, xp)], axis=0)
                parts.append(_dot(jnp.concatenate(ms, axis=1), rhs))
            y_diag = jnp.concatenate(parts, axis=1)
            st_prev = state_scr[:, gcols]
            y_off = _dot(cg_, st_prev.astype(BF16)) * e_decay_x[:, gcols]
            st_new = _dot_tn(bg_, xd_b[:, gcols])
            state_scr[:, gcols] = st_prev * e_decay_x[L - 1:L, gcols] + st_new
            yg = y_diag + y_off + dexp[:, gcols] * xs[:, gcols]
            yg = yg * _silu(z_scr[pl.ds(r0, L), gcols])
            yg = yg * lax.rsqrt(jnp.mean(yg * yg, axis=-1, keepdims=True) + EPS)
            ys.append((yg * nrm[:, gcols]).astype(BF16))
        y_scr[pl.ds(r0, L), :] = jnp.concatenate(ys, axis=1)
        return carry

    lax.fori_loop(0, rows // L, ssd_chunk, 0)
    out = _dot(y_scr[...], wout_ref[...])
    o_ref[...] = x + mod[2:3, :] * out


def _ssd_layer(x, mod, ng, wz, wxbc, wdt, cw, cb, dtb, alog, dexp, nrm, expand, wout, *, rows=512):
    bsz, s, d = x.shape
    din = wz.shape[1]
    cdim = wxbc.shape[1]
    res = [ng, wz, wxbc, wdt, cw, cb, dtb, alog, dexp, nrm, expand, wout]
    return pl.pallas_call(
        _ssd_body,
        out_shape=jax.ShapeDtypeStruct(x.shape, F32),
        grid=(bsz, s // rows),
        in_specs=[
            pl.BlockSpec((None, rows, d), lambda b, j: (b, j, 0)),
            pl.BlockSpec((None, 6, d), lambda b, j: (b, 0, 0)),
        ] + [_resident(a.shape) for a in res],
        out_specs=pl.BlockSpec((None, rows, d), lambda b, j: (b, j, 0)),
        scratch_shapes=[
            pltpu.VMEM((rows, din), F32),
            pltpu.VMEM((V7X_SUBLANES + rows, cdim), F32),
            pltpu.VMEM((rows, cdim), F32),
            pltpu.VMEM((rows, V7X_LANES), F32),
            pltpu.VMEM((rows, din), BF16),
            pltpu.VMEM((SSD_STATE, din), F32),
        ],
        compiler_params=_params(("parallel", "arbitrary")),
        name="ssd_layer",
    )(x, mod, *res)


def _route(logits_t, n_groups):
    ne, tm = logits_t.shape
    per = ne // n_groups
    ex = jnp.exp(logits_t - jnp.max(logits_t, axis=0, keepdims=True))
    probs = ex / jnp.sum(ex, axis=0, keepdims=True)
    p3 = probs.reshape(n_groups, per, tm)
    io = lax.broadcasted_iota(jnp.int32, p3.shape, 1)
    m1 = jnp.max(p3, axis=1, keepdims=True)
    sel1 = io == jnp.min(jnp.where(p3 == m1, io, per), axis=1, keepdims=True)
    rest = jnp.where(sel1, -1.0, p3)
    m2 = jnp.max(rest, axis=1, keepdims=True)
    sel2 = io == jnp.min(jnp.where(rest == m2, io, per), axis=1, keepdims=True)
    score = m1 + m2
    gio = lax.broadcasted_iota(jnp.int32, score.shape, 0)
    best = jnp.min(jnp.where(score == jnp.max(score, axis=0, keepdims=True), gio, n_groups),
                   axis=0, keepdims=True)
    top = jnp.where(sel1, m1, 0.0) + jnp.where(sel2, m2, 0.0)
    in_best = gio == best
    gates = jnp.sum(jnp.where(in_best, top / score, 0.0), axis=0)
    return gates, in_best.astype(F32).reshape(n_groups, tm)


def _copy_rows(n, src_ref, src0, dst_ref, dst0):
    size = MOE_SUB
    while size >= ROW_ALIGN:
        if size == MOE_SUB:
            take, off = n >= size, 0
        else:
            take, off = (n & size) != 0, n & ~(2 * size - 1)

        @pl.when(take)
        def _(size=size, off=off):
            s0 = pl.multiple_of(src0 + off, ROW_ALIGN)
            d0 = pl.multiple_of(dst0 + off, ROW_ALIGN)
            dst_ref[pl.ds(d0, size), :] = src_ref[pl.ds(s0, size), :]
        size //= 2


def _moe_body(x_ref, mod_ref, ng_ref, rwh_ref, rwl_ref, rb_ref, wg_ref, wu_ref, wd_ref, fg_ref,
              o_ref, reg_scr, greg_scr, perm_scr, srt_scr, gsrt_scr, tab_smem, cnt_smem, *, final_norm):
    grp = pl.program_id(2)
    tm, d = x_ref.shape
    per = wg_ref.shape[0]
    ngroups = N_EXPERT_GROUPS
    nsub = tm // MOE_SUB
    sp = perm_scr.shape[1]
    mod = mod_ref[...]

    first = (pl.program_id(0) == 0) & (pl.program_id(1) == 0) & (grp == 0)

    @pl.when(first)
    def _():
        reg_scr[...] = jnp.zeros_like(reg_scr)
        greg_scr[...] = jnp.zeros_like(greg_scr)

    @pl.when(grp == 0)
    def _():
        for k in range(ngroups):
            cnt_smem[k] = 0
        tok_i = lax.broadcasted_iota(jnp.int32, (MOE_SUB, MOE_SUB), 0)
        tok_j = lax.broadcasted_iota(jnp.int32, (MOE_SUB, MOE_SUB), 1)
        triu = (tok_i <= tok_j).astype(BF16)
        row_i = lax.broadcasted_iota(jnp.int32, (sp, MOE_SUB), 0).astype(F32)
        for s in range(nsub):
            rows = slice(s * MOE_SUB, (s + 1) * MOE_SUB)
            hn = _norm_mod(x_ref[rows, :], ng_ref[...], mod[3:4, :], mod[4:5, :])
            hi = hn.astype(BF16)
            lo = (hn - hi.astype(F32)).astype(BF16)
            logits = _dot(hi, rwh_ref[...]) + _dot(lo, rwh_ref[...]) + _dot(hi, rwl_ref[...]) + rb_ref[...]
            gates4, member = _route(logits.T[0:ngroups * per, :], ngroups)
            cum = _dot(member.astype(BF16), triu)
            cnt = cum[:, MOE_SUB - 1:MOE_SUB]
            c16 = jnp.floor((cnt + (ROW_ALIGN - 1.0)) * (1.0 / ROW_ALIGN)) * ROW_ALIGN
            bases = [jnp.zeros((1, 1), F32)]
            for k in range(1, ngroups):
                bases.append(bases[-1] + c16[k - 1:k, :])
            base = jnp.concatenate(bases, axis=0)
            pos = jnp.sum(member * (cum - 1.0 + base), axis=0, keepdims=True)
            onehot = row_i == pos
            pb = jnp.where(onehot, 1.0, 0.0).astype(BF16)
            perm_scr[s] = pb
            srt_scr[...] = _dot(pb, hi).astype(BF16)
            g_pad = jnp.concatenate([gates4, jnp.zeros((V7X_LANES - per, MOE_SUB), F32)], axis=0)
            gsrt_scr[...] = _dot_nt(jnp.where(onehot, 1.0, 0.0), g_pad, precision=lax.Precision.HIGHEST)
            for k in range(ngroups):
                n = c16[k, 0].astype(jnp.int32)
                b0 = base[k, 0].astype(jnp.int32)
                r0 = cnt_smem[k]
                t0 = (s * ngroups + k) * 3
                tab_smem[t0] = b0
                tab_smem[t0 + 1] = r0
                tab_smem[t0 + 2] = n
                _copy_rows(n, srt_scr, b0, reg_scr.at[k], r0)
                _copy_rows(n, gsrt_scr, b0, greg_scr.at[k], r0)
                cnt_smem[k] = r0 + n

    def experts(off, m):
        h = reg_scr[grp, pl.ds(off, m), :]
        gt = greg_scr[grp, pl.ds(off, m), :]
        acts = []
        for e in range(per):
            a = _silu(_dot(h, wg_ref[e])) * _dot(h, wu_ref[e]) * gt[:, e:e + 1]
            acts.append(a.astype(BF16))
        out = _dot(jnp.concatenate(acts, axis=1), wd_ref[...])
        reg_scr[grp, pl.ds(off, m), :] = out.astype(BF16)

    n_rows = cnt_smem[grp]
    n_full = n_rows // MOE_CHUNK

    def full_chunk(ci, carry):
        experts(pl.multiple_of(ci * MOE_CHUNK, MOE_CHUNK), MOE_CHUNK)
        return carry

    lax.fori_loop(0, n_full, full_chunk, 0)
    rem = n_rows - n_full * MOE_CHUNK
    tail = pl.multiple_of(n_full * MOE_CHUNK, MOE_CHUNK)

    @pl.when(rem > MOE_CHUNK // 2)
    def _():
        experts(tail, MOE_CHUNK)

    @pl.when((rem > 0) & (rem <= MOE_CHUNK // 2))
    def _():
        experts(tail, MOE_CHUNK // 2)

    @pl.when(grp == ngroups - 1)
    def _():
        for s in range(nsub):
            rows = slice(s * MOE_SUB, (s + 1) * MOE_SUB)
            for k in range(ngroups):
                t0 = (s * ngroups + k) * 3
                _copy_rows(tab_smem[t0 + 2], reg_scr.at[k], tab_smem[t0 + 1], srt_scr, tab_smem[t0])
            out = _dot_tn(perm_scr[s], srt_scr[...])
            y = x_ref[rows, :] + mod[5:6, :] * out
            if final_norm:
                y = y * lax.rsqrt(jnp.mean(y * y, axis=-1, keepdims=True) + EPS) * fg_ref[...]
            o_ref[rows, :] = y


def _moe_layer(x, mod, ng, rwh, rwl, rb, wg, wu, wd, fg, *, final_norm, tm=1024):
    bsz, s, d = x.shape
    ne, _, de = wg.shape
    ngroups = N_EXPERT_GROUPS
    per = ne // ngroups
    nsub = tm // MOE_SUB
    sp = MOE_SUB + ngroups * ROW_ALIGN
    cap = tm + nsub * ngroups * ROW_ALIGN + MOE_CHUNK
    small = [ng, rwh, rwl, rb]
    return pl.pallas_call(
        functools.partial(_moe_body, final_norm=final_norm),
        out_shape=jax.ShapeDtypeStruct(x.shape, F32),
        grid=(bsz, s // tm, ngroups),
        in_specs=[
            pl.BlockSpec((None, tm, d), lambda b, j, g: (b, j, 0)),
            pl.BlockSpec((None, 6, d), lambda b, j, g: (b, 0, 0)),
        ] + [_resident(a.shape) for a in small] + [
            pl.BlockSpec((per, d, de), lambda b, j, g: (g, 0, 0)),
            pl.BlockSpec((per, d, de), lambda b, j, g: (g, 0, 0)),
            pl.BlockSpec((None, per * de, d), lambda b, j, g: (g, 0, 0)),
            _resident(fg.shape),
        ],
        out_specs=pl.BlockSpec((None, tm, d), lambda b, j, g: (b, j, 0)),
        scratch_shapes=[
            pltpu.VMEM((ngroups, cap, d), BF16),
            pltpu.VMEM((ngroups, cap, V7X_LANES), F32),
            pltpu.VMEM((nsub, sp, MOE_SUB), BF16),
            pltpu.VMEM((sp, d), BF16),
            pltpu.VMEM((sp, V7X_LANES), F32),
            pltpu.SMEM((nsub * ngroups * 3,), jnp.int32),
            pltpu.SMEM((ngroups,), jnp.int32),
        ],
        compiler_params=_params(("arbitrary", "arbitrary", "arbitrary")),
        name="moe_layer",
    )(x, mod, *small, wg, wu, wd.reshape(ngroups, per * de, d), fg)


def kernel(x, c, ada_w, ada_b, norm1_g, norm2_g, lru_w_in, lru_conv_w, lru_conv_b, lru_wa, lru_ba, lru_wx, lru_bx, lru_lambda, lru_w_out, ssd_w_in, ssd_conv_w, ssd_conv_b, ssd_dt_bias, ssd_a_log, ssd_d, ssd_norm_g, ssd_w_out, router_w, router_b, moe_w_gate, moe_w_up, moe_w_down, final_norm_g):
    bsz, s, d = x.shape
    depth = ada_w.shape[0]
    n_mixers = 2
    mod = _ada_mod(c, ada_w, ada_b).reshape(depth, bsz, 6, d)

    heads = ssd_dt_bias.shape[1]
    din = heads * SSD_HEAD_DIM
    cdim = din + 2 * SSD_GROUPS * SSD_STATE
    lane_pad = V7X_LANES - heads
    expand = (jnp.arange(V7X_LANES)[:, None] == (jnp.arange(din) // SSD_HEAD_DIM)[None, :]).astype(BF16)

    ne = router_w.shape[1]
    rw = jnp.pad(router_w, ((0, 0), (0, V7X_LANES - ne)))
    rwh = rw.astype(BF16)
    rwl = (rw - rwh.astype(F32)).astype(BF16)
    rb = jnp.pad(router_b, (0, V7X_LANES - ne)).reshape(1, V7X_LANES)
    fg = final_norm_g.reshape(1, d)

    for i in range(depth):
        j = i // n_mixers
        ng1 = norm1_g[i].reshape(1, d)
        if i % n_mixers == 0:
            nh = lru_wa.shape[1]
            w = lru_w_out.shape[1]
            wg = jnp.concatenate([lru_wa[j], lru_wx[j]], axis=-1).astype(BF16)
            bg = jnp.concatenate([lru_ba[j].reshape(nh, 1, w // nh), lru_bx[j].reshape(nh, 1, w // nh)], axis=-1)
            x = _lru_layer(x, mod[i], ng1, lru_w_in[j].astype(BF16), lru_conv_w[j], lru_conv_b[j].reshape(1, w),
                           wg, bg, lru_lambda[j].reshape(1, w), lru_w_out[j].astype(BF16))
        else:
            w_in = ssd_w_in[j]
            wz = w_in[:, :din].astype(BF16)
            wxbc = w_in[:, din:din + cdim].astype(BF16)
            wdt = jnp.pad(w_in[:, din + cdim:], ((0, 0), (0, lane_pad))).astype(BF16)
            dtb = jnp.pad(ssd_dt_bias[j], (0, lane_pad)).reshape(1, V7X_LANES)
            alog = jnp.pad(ssd_a_log[j], (0, lane_pad)).reshape(1, V7X_LANES)
            dexp = jnp.repeat(ssd_d[j], SSD_HEAD_DIM).reshape(1, din)
            x = _ssd_layer(x, mod[i], ng1, wz, wxbc, wdt, ssd_conv_w[j], ssd_conv_b[j].reshape(1, cdim),
                           dtb, alog, dexp, ssd_norm_g[j].reshape(1, din), expand, ssd_w_out[j].astype(BF16))
        x = _moe_layer(x, mod[i], norm2_g[i].reshape(1, d), rwh, rwl, rb,
                       moe_w_gate[i].astype(BF16), moe_w_up[i].astype(BF16), moe_w_down[i].astype(BF16), fg,
                       final_norm=(i == depth - 1))
    return x
```

```python
import functools

import jax
import jax.numpy as jnp
from jax import lax
from jax.experimental import pallas as pl
from jax.experimental.pallas import tpu as pltpu

F32 = jnp.float32
BF16 = jnp.bfloat16

EPS = 1e-6
LRU_C = 8.0
SSD_CHUNK = 128
SSD_HEAD_DIM = 64
SSD_GROUPS = 4
SSD_STATE = 128
N_EXPERT_GROUPS = 4
TOP_K = 2

V7X_LANES = 128
V7X_SUBLANES = 8
V7X_VMEM_LIMIT_BYTES = 56 * 1024 * 1024
V7X_BF16_ROWS = 16

MOE_SUB = 256
ROW_ALIGN = V7X_BF16_ROWS
MOE_CHUNK = 256


def _params(semantics):
    return pltpu.CompilerParams(dimension_semantics=semantics, vmem_limit_bytes=V7X_VMEM_LIMIT_BYTES)


def _resident(shape):
    return pl.BlockSpec(shape, lambda *_: (0,) * len(shape), pipeline_mode=pl.Buffered(1))


def _norm_mod(x, g, shift, scale):
    y = x * lax.rsqrt(jnp.mean(x * x, axis=-1, keepdims=True) + EPS)
    return (y * g) * (1.0 + scale) + shift


def _softplus(z):
    return jnp.maximum(z, 0.0) + jnp.log1p(jnp.exp(-jnp.abs(z)))


def _sigmoid(z):
    return 0.5 + 0.5 * jnp.tanh(0.5 * z)


def _silu(z):
    return z * _sigmoid(z)


def _dot(a, b):
    return jnp.dot(a, b, preferred_element_type=F32)


def _dot_nt(a, b, precision=None):
    return lax.dot_general(a, b, (((1,), (1,)), ((), ())), precision=precision, preferred_element_type=F32)


def _dot_tn(a, b):
    return lax.dot_general(a, b, (((0,), (0,)), ((), ())), preferred_element_type=F32)


def _ada_body(c_ref, w_ref, b_ref, o_ref):
    cond = _silu(c_ref[...]).astype(BF16)
    o_ref[...] = _dot(cond, w_ref[...].astype(BF16)) + b_ref[...]


def _ada_mod(c, ada_w, ada_b, tn=1536):
    depth, d, n = ada_w.shape
    bsz = c.shape[0]
    return pl.pallas_call(
        _ada_body,
        out_shape=jax.ShapeDtypeStruct((depth, bsz, n), F32),
        grid=(depth, n // tn),
        in_specs=[
            pl.BlockSpec((bsz, d), lambda l, j: (0, 0)),
            pl.BlockSpec((None, d, tn), lambda l, j: (l, 0, j)),
            pl.BlockSpec((None, 1, tn), lambda l, j: (l, 0, j)),
        ],
        out_specs=pl.BlockSpec((None, bsz, tn), lambda l, j: (l, 0, j)),
        compiler_params=_params(("parallel", "parallel")),
        name="ada_mod",
    )(c, ada_w, ada_b.reshape(depth, 1, n))


def _lru_body(x_ref, mod_ref, ng_ref, win_ref, cw_ref, cb_ref, wg_ref, bg_ref, lam_ref, wout_ref,
              o_ref, xpad_scr, h_scr, *, tc):
    nb, ts, d = x_ref.shape
    w = xpad_scr.shape[1]
    rc = nb * tc
    nh = wg_ref.shape[0]
    hw = w // nh
    kc = cw_ref.shape[0]
    halo = (kc - 1) * nb

    @pl.when(pl.program_id(0) == 0)
    def _():
        xpad_scr[0:halo, :] = jnp.zeros((halo, w), F32)
        h_scr[...] = jnp.zeros_like(h_scr)

    mod = mod_ref[...]
    ng = ng_ref[...]
    cw = cw_ref[...]
    cb = cb_ref[...]
    sp = _softplus(-lam_ref[...])

    def chunk(ci, h):
        t0 = pl.multiple_of(ci * tc, tc)
        x = x_ref[:, pl.ds(t0, tc), :]
        hn = _norm_mod(x, ng, mod[:, 0:1, :], mod[:, 1:2, :])
        u = _dot(hn.reshape(rc, d).astype(BF16), win_ref[...])
        gate = u[:, :w]
        xpad_scr[halo:halo + rc, :] = jnp.swapaxes(u[:, w:].reshape(nb, tc, w), 0, 1).reshape(rc, w)
        xc = cb
        for k in range(kc):
            xc = xc + cw[k:k + 1, :] * xpad_scr[k * nb:k * nb + rc, :]
        xpad_scr[0:halo, :] = xpad_scr[rc:rc + halo, :]
        a_parts, b_parts = [], []
        for hd in range(nh):
            cols = slice(hd * hw, (hd + 1) * hw)
            xh = xc[:, cols]
            ri = _dot(xh.astype(BF16), wg_ref[hd]) + bg_ref[hd]
            r = _sigmoid(ri[:, :hw])
            i = _sigmoid(ri[:, hw:])
            log_a = (-LRU_C) * r * sp[:, cols]
            t = jnp.tanh(log_a)
            a_parts.append(jnp.exp(log_a))
            b_parts.append(jnp.sqrt(-2.0 * t / (1.0 - t)) * (i * xh))
        a = jnp.concatenate(a_parts, axis=1)
        b = jnp.concatenate(b_parts, axis=1)
        hs = []
        for ti in range(tc):
            h = a[ti * nb:(ti + 1) * nb, :] * h + b[ti * nb:(ti + 1) * nb, :]
            hs.append(h)
        hs = jnp.swapaxes(jnp.concatenate(hs, axis=0).reshape(tc, nb, w), 0, 1).reshape(rc, w)
        y = (hs * jax.nn.gelu(gate, approximate=True)).astype(BF16)
        out = _dot(y, wout_ref[...]).reshape(nb, tc, d)
        o_ref[:, pl.ds(t0, tc), :] = x + mod[:, 2:3, :] * out
        return h

    h_scr[...] = lax.fori_loop(0, ts // tc, chunk, h_scr[...])


def _lru_layer(x, mod, ng, win, cw, cb, wg, bg, lam, wout, *, ts=128, tc=16):
    bsz, s, d = x.shape
    w = wout.shape[0]
    kc = cw.shape[0]
    return pl.pallas_call(
        functools.partial(_lru_body, tc=tc),
        out_shape=jax.ShapeDtypeStruct(x.shape, F32),
        grid=(s // ts,),
        in_specs=[
            pl.BlockSpec((bsz, ts, d), lambda j: (0, j, 0)),
            _resident(mod.shape), _resident(ng.shape), _resident(win.shape), _resident(cw.shape),
            _resident(cb.shape), _resident(wg.shape), _resident(bg.shape), _resident(lam.shape),
            _resident(wout.shape),
        ],
        out_specs=pl.BlockSpec((bsz, ts, d), lambda j: (0, j, 0)),
        scratch_shapes=[
            pltpu.VMEM(((kc - 1 + tc) * bsz, w), F32),
            pltpu.VMEM((bsz, w), F32),
        ],
        compiler_params=_params(("arbitrary",)),
        name="lru_layer",
    )(x, mod, ng, win, cw, cb, wg, bg, lam, wout)


def _ssd_body(x_ref, mod_ref, ng_ref, wz_ref, wxbc_ref, wdt_ref, cw_ref, cb_ref, dtb_ref, alog_ref,
              dexp_ref, nrm_ref, expand_ref, wout_ref,
              o_ref, z_scr, xpad_scr, act_scr, state_scr):
    rows, d = x_ref.shape
    din = z_scr.shape[1]
    cdim = act_scr.shape[1]
    kc = cw_ref.shape[0]
    halo = V7X_SUBLANES
    L, P, G, N = SSD_CHUNK, SSD_HEAD_DIM, SSD_GROUPS, SSD_STATE
    gw = din // G
    hpg = gw // P

    @pl.when(pl.program_id(1) == 0)
    def _():
        xpad_scr[0:halo, :] = jnp.zeros((halo, cdim), F32)
        state_scr[...] = jnp.zeros_like(state_scr)

    mod = mod_ref[...]
    ng = ng_ref[...]
    cw = cw_ref[...]
    cb = cb_ref[...]
    neg_a = -jnp.exp(alog_ref[...])
    dtb = dtb_ref[...]
    dexp = dexp_ref[...]
    nrm = nrm_ref[...]
    expand = expand_ref[...]
    row_i = lax.broadcasted_iota(jnp.int32, (L, L), 0)
    col_i = lax.broadcasted_iota(jnp.int32, (L, L), 1)
    causal = row_i >= col_i
    tril = causal.astype(F32)
    lane_lo = lax.broadcasted_iota(jnp.int32, (L, 2 * P), 1) < P

    def ssd_chunk(ci, carry):
        r0 = pl.multiple_of(ci * L, L)
        x = x_ref[pl.ds(r0, L), :]
        hn = _norm_mod(x, ng, mod[0:1, :], mod[1:2, :]).astype(BF16)
        z_scr[...] = _dot(hn, wz_ref[...])
        xpad_scr[halo:halo + L, :] = _dot(hn, wxbc_ref[...])
        dt_pre = _dot(hn, wdt_ref[...])
        win = xpad_scr[...]
        acc = cb + cw[kc - 1:kc, :] * win[halo:halo + L, :]
        for k in range(kc - 1):
            acc = acc + cw[k:k + 1, :] * pltpu.roll(win, kc - 1 - k, axis=0)[halo:halo + L, :]
        act_scr[...] = _silu(acc)
        xpad_scr[0:halo, :] = xpad_scr[L:L + halo, :]
        dt = _softplus(dt_pre + dtb)
        da = dt * neg_a
        acs = jnp.dot(tril, da, precision=lax.Precision.HIGHEST, preferred_element_type=F32)
        acs_t = acs.T
        dt_t = dt.T
        last = acs[L - 1:L, :]
        e_decay = jnp.exp(acs)
        w_state = dt * jnp.exp(last - acs)
        ex = _dot(jnp.concatenate([e_decay, w_state], axis=0).astype(BF16), expand)
        e_decay_x = ex[:L]
        w_state_x = ex[L:]
        xs = act_scr[:, 0:din]
        xs_b = xs.astype(BF16)
        xd_b = (xs * w_state_x).astype(BF16)
        ys = []
        for g in range(G):
            bg_ = act_scr[:, din + g * N: din + (g + 1) * N].astype(BF16)
            cg_ = act_scr[:, din + (G + g) * N: din + (G + g + 1) * N].astype(BF16)
            cb_ = _dot_nt(cg_, bg_)
            gcols = slice(g * gw, (g + 1) * gw)
            parts = []
            for hp in range(hpg // 2):
                ms = []
                for hh in range(2):
                    h = g * hpg + 2 * hp + hh
                    seg = acs[:, h:h + 1] - acs_t[h:h + 1, :]
                    lmat = jnp.where(causal, jnp.exp(seg), 0.0)
                    ms.append((cb_ * lmat * dt_t[h:h + 1, :]).astype(BF16))
                c0 = g * gw + hp * 2 * P
                xp = xs_b[:, c0:c0 + 2 * P]
                zero = jnp.zeros_like(xp)
                rhs = jnp.concatenate([jnp.where(lane_lo, xp, zero), jnp.where(lane_lo, zero, xp)], axis=0)
                parts.append(_dot(jnp.concatenate(ms, axis=1), rhs))
            y_diag = jnp.concatenate(parts, axis=1)
            st_prev = state_scr[:, gcols]
            y_off = _dot(cg_, st_prev.astype(BF16)) * e_decay_x[:, gcols]
            st_new = _dot_tn(bg_, xd_b[:, gcols])
            state_scr[:, gcols] = st_prev * e_decay_x[L - 1:L, gcols] + st_new
            yg = y_diag + y_off + dexp[:, gcols] * xs[:, gcols]
            yg = yg * _silu(z_scr[:, gcols])
            yg = yg * lax.rsqrt(jnp.mean(yg * yg, axis=-1, keepdims=True) + EPS)
            ys.append((yg * nrm[:, gcols]).astype(BF16))
        out = _dot(jnp.concatenate(ys, axis=1), wout_ref[...])
        o_ref[pl.ds(r0, L), :] = x + mod[2:3, :] * out
        return carry

    lax.fori_loop(0, rows // L, ssd_chunk, 0)


def _ssd_layer(x, mod, ng, wz, wxbc, wdt, cw, cb, dtb, alog, dexp, nrm, expand, wout, *, rows=512):
    bsz, s, d = x.shape
    din = wz.shape[1]
    cdim = wxbc.shape[1]
    res = [ng, wz, wxbc, wdt, cw, cb, dtb, alog, dexp, nrm, expand, wout]
    return pl.pallas_call(
        _ssd_body,
        out_shape=jax.ShapeDtypeStruct(x.shape, F32),
        grid=(bsz, s // rows),
        in_specs=[
            pl.BlockSpec((None, rows, d), lambda b, j: (b, j, 0)),
            pl.BlockSpec((None, 6, d), lambda b, j: (b, 0, 0)),
        ] + [_resident(a.shape) for a in res],
        out_specs=pl.BlockSpec((None, rows, d), lambda b, j: (b, j, 0)),
        scratch_shapes=[
            pltpu.VMEM((SSD_CHUNK, din), F32),
            pltpu.VMEM((V7X_SUBLANES + SSD_CHUNK, cdim), F32),
            pltpu.VMEM((SSD_CHUNK, cdim), F32),
            pltpu.VMEM((SSD_STATE, din), F32),
        ],
        compiler_params=_params(("parallel", "arbitrary")),
        name="ssd_layer",
    )(x, mod, *res)


def _route(logits_t, n_groups):
    ne, tm = logits_t.shape
    per = ne // n_groups
    ex = jnp.exp(logits_t - jnp.max(logits_t, axis=0, keepdims=True))
    probs = ex / jnp.sum(ex, axis=0, keepdims=True)
    p3 = probs.reshape(n_groups, per, tm)
    io = lax.broadcasted_iota(jnp.int32, p3.shape, 1)
    m1 = jnp.max(p3, axis=1, keepdims=True)
    sel1 = io == jnp.min(jnp.where(p3 == m1, io, per), axis=1, keepdims=True)
    rest = jnp.where(sel1, -1.0, p3)
    m2 = jnp.max(rest, axis=1, keepdims=True)
    sel2 = io == jnp.min(jnp.where(rest == m2, io, per), axis=1, keepdims=True)
    score = m1 + m2
    gio = lax.broadcasted_iota(jnp.int32, score.shape, 0)
    best = jnp.min(jnp.where(score == jnp.max(score, axis=0, keepdims=True), gio, n_groups),
                   axis=0, keepdims=True)
    top = jnp.where(sel1, m1, 0.0) + jnp.where(sel2, m2, 0.0)
    in_best = gio == best
    gates = jnp.sum(jnp.where(in_best, top / score, 0.0), axis=0)
    return gates, in_best.astype(F32).reshape(n_groups, tm)


def _moe_body(x_ref, mod_ref, ng_ref, rw2_ref, rwh_ref, rb_ref, wg_ref, wu_ref, wd_ref, fg_ref,
              o_ref, reg_scr, greg_scr, perm_scr, srt_scr, gsrt_scr, tab_smem, cnt_smem, *, final_norm):
    grp = pl.program_id(2)
    tm, d = x_ref.shape
    per = wg_ref.shape[0]
    ngroups = N_EXPERT_GROUPS
    nsub = tm // MOE_SUB
    sp = perm_scr.shape[1]
    mod = mod_ref[...]

    first = (pl.program_id(0) == 0) & (pl.program_id(1) == 0) & (grp == 0)

    @pl.when(first)
    def _():
        reg_scr[...] = jnp.zeros_like(reg_scr)
        greg_scr[...] = jnp.zeros_like(greg_scr)
        srt_scr[...] = jnp.zeros_like(srt_scr)
        gsrt_scr[...] = jnp.zeros_like(gsrt_scr)

    @pl.when(grp == 0)
    def _():
        for k in range(ngroups):
            cnt_smem[k] = 0
        tok_i = lax.broadcasted_iota(jnp.int32, (MOE_SUB, MOE_SUB), 0)
        tok_j = lax.broadcasted_iota(jnp.int32, (MOE_SUB, MOE_SUB), 1)
        triu = (tok_i <= tok_j).astype(BF16)
        row_i = lax.broadcasted_iota(jnp.int32, (sp, MOE_SUB), 0).astype(F32)
        for s in range(nsub):
            rows = slice(s * MOE_SUB, (s + 1) * MOE_SUB)
            hn = _norm_mod(x_ref[rows, :], ng_ref[...], mod[3:4, :], mod[4:5, :])
            hi = hn.astype(BF16)
            lo = (hn - hi.astype(F32)).astype(BF16)
            l2 = _dot(hi, rw2_ref[...])
            logits = l2[:, :V7X_LANES] + l2[:, V7X_LANES:] + _dot(lo, rwh_ref[...]) + rb_ref[...]
            gates4, member = _route(logits.T[0:ngroups * per, :], ngroups)
            cum = _dot(member.astype(BF16), triu)
            cnt = cum[:, MOE_SUB - 1:MOE_SUB]
            c16 = jnp.floor((cnt + (ROW_ALIGN - 1.0)) * (1.0 / ROW_ALIGN)) * ROW_ALIGN
            bases = [jnp.zeros((1, 1), F32)]
            for k in range(1, ngroups):
                bases.append(bases[-1] + c16[k - 1:k, :])
            base = jnp.concatenate(bases, axis=0)
            pos = jnp.sum(member * (cum - 1.0 + base), axis=0, keepdims=True)
            pb = jnp.where(row_i == pos, 1.0, 0.0).astype(BF16)
            perm_scr[s] = pb
            srt_scr[s, 0:sp, :] = _dot(pb, hi).astype(BF16)
            g_hi = gates4.astype(BF16).astype(F32)
            g_cat = jnp.concatenate(
                [g_hi, gates4 - g_hi, jnp.zeros((V7X_LANES - 2 * per, MOE_SUB), F32)], axis=0).astype(BF16)
            gsrt_scr[s, 0:sp, :] = _dot_nt(pb, g_cat)
            for k in range(ngroups):
                b0 = pl.multiple_of(base[k, 0].astype(jnp.int32), ROW_ALIGN)
                r0 = pl.multiple_of(cnt_smem[k], ROW_ALIGN)
                t0 = (s * ngroups + k) * 2
                tab_smem[t0] = b0
                tab_smem[t0 + 1] = r0
                reg_scr[k, pl.ds(r0, MOE_SUB), :] = srt_scr[s, pl.ds(b0, MOE_SUB), :]
                greg_scr[k, pl.ds(r0, MOE_SUB), :] = gsrt_scr[s, pl.ds(b0, MOE_SUB), :]
                cnt_smem[k] = r0 + c16[k, 0].astype(jnp.int32)

    def experts(off, m):
        h = reg_scr[grp, pl.ds(off, m), :]
        gt = greg_scr[grp, pl.ds(off, m), :]
        acts = []
        for e in range(per):
            gate = gt[:, e:e + 1] + gt[:, per + e:per + e + 1]
            a = _silu(_dot(h, wg_ref[e])) * _dot(h, wu_ref[e]) * gate
            acts.append(a.astype(BF16))
        out = _dot(jnp.concatenate(acts, axis=1), wd_ref[...])
        reg_scr[grp, pl.ds(off, m), :] = out.astype(BF16)

    n_rows = cnt_smem[grp]
    n_full = n_rows // MOE_CHUNK

    def full_chunk(ci, carry):
        experts(pl.multiple_of(ci * MOE_CHUNK, MOE_CHUNK), MOE_CHUNK)
        return carry

    lax.fori_loop(0, n_full, full_chunk, 0)
    rem = n_rows - n_full * MOE_CHUNK
    tail = pl.multiple_of(n_full * MOE_CHUNK, MOE_CHUNK)

    @pl.when(rem > MOE_CHUNK // 2)
    def _():
        experts(tail, MOE_CHUNK)

    @pl.when((rem > 0) & (rem <= MOE_CHUNK // 2))
    def _():
        experts(tail, MOE_CHUNK // 2)

    @pl.when(grp == ngroups - 1)
    def _():
        for s in range(nsub):
            rows = slice(s * MOE_SUB, (s + 1) * MOE_SUB)
            for k in range(ngroups):
                t0 = (s * ngroups + k) * 2
                b0 = pl.multiple_of(tab_smem[t0], ROW_ALIGN)
                r0 = pl.multiple_of(tab_smem[t0 + 1], ROW_ALIGN)
                srt_scr[s, pl.ds(b0, MOE_SUB), :] = reg_scr[k, pl.ds(r0, MOE_SUB), :]
            out = _dot_tn(perm_scr[s], srt_scr[s, 0:sp, :])
            y = x_ref[rows, :] + mod[5:6, :] * out
            if final_norm:
                y = y * lax.rsqrt(jnp.mean(y * y, axis=-1, keepdims=True) + EPS) * fg_ref[...]
            o_ref[rows, :] = y


def _moe_layer(x, mod, ng, rw2, rwh, rb, wg, wu, wd, fg, *, final_norm, tm=1024):
    bsz, s, d = x.shape
    ne, _, de = wg.shape
    ngroups = N_EXPERT_GROUPS
    per = ne // ngroups
    nsub = tm // MOE_SUB
    sp = MOE_SUB + ngroups * ROW_ALIGN
    cap = tm + nsub * ngroups * ROW_ALIGN + max(MOE_CHUNK, MOE_SUB)
    small = [ng, rw2, rwh, rb]
    return pl.pallas_call(
        functools.partial(_moe_body, final_norm=final_norm),
        out_shape=jax.ShapeDtypeStruct(x.shape, F32),
        grid=(bsz, s // tm, ngroups),
        in_specs=[
            pl.BlockSpec((None, tm, d), lambda b, j, g: (b, j, 0)),
            pl.BlockSpec((None, 6, d), lambda b, j, g: (b, 0, 0)),
        ] + [_resident(a.shape) for a in small] + [
            pl.BlockSpec((per, d, de), lambda b, j, g: (g, 0, 0)),
            pl.BlockSpec((per, d, de), lambda b, j, g: (g, 0, 0)),
            pl.BlockSpec((None, per * de, d), lambda b, j, g: (g, 0, 0)),
            _resident(fg.shape),
        ],
        out_specs=pl.BlockSpec((None, tm, d), lambda b, j, g: (b, j, 0)),
        scratch_shapes=[
            pltpu.VMEM((ngroups, cap, d), BF16),
            pltpu.VMEM((ngroups, cap, V7X_LANES), F32),
            pltpu.VMEM((nsub, sp, MOE_SUB), BF16),
            pltpu.VMEM((nsub, sp + MOE_SUB, d), BF16),
            pltpu.VMEM((nsub, sp + MOE_SUB, V7X_LANES), F32),
            pltpu.SMEM((nsub * ngroups * 2,), jnp.int32),
            pltpu.SMEM((ngroups,), jnp.int32),
        ],
        compiler_params=_params(("arbitrary", "arbitrary", "arbitrary")),
        name="moe_layer",
    )(x, mod, *small, wg, wu, wd.reshape(ngroups, per * de, d), fg)


def kernel(x, c, ada_w, ada_b, norm1_g, norm2_g, lru_w_in, lru_conv_w, lru_conv_b, lru_wa, lru_ba, lru_wx, lru_bx, lru_lambda, lru_w_out, ssd_w_in, ssd_conv_w, ssd_conv_b, ssd_dt_bias, ssd_a_log, ssd_d, ssd_norm_g, ssd_w_out, router_w, router_b, moe_w_gate, moe_w_up, moe_w_down, final_norm_g):
    bsz, s, d = x.shape
    depth = ada_w.shape[0]
    n_mixers = 2
    mod = _ada_mod(c, ada_w, ada_b).reshape(depth, bsz, 6, d)

    heads = ssd_dt_bias.shape[1]
    din = heads * SSD_HEAD_DIM
    cdim = din + 2 * SSD_GROUPS * SSD_STATE
    lane_pad = V7X_LANES - heads
    expand = (jnp.arange(V7X_LANES)[:, None] == (jnp.arange(din) // SSD_HEAD_DIM)[None, :]).astype(BF16)

    ne = router_w.shape[1]
    rw = jnp.pad(router_w, ((0, 0), (0, V7X_LANES - ne)))
    rwh = rw.astype(BF16)
    rw2 = jnp.concatenate([rwh, (rw - rwh.astype(F32)).astype(BF16)], axis=1)
    rb = jnp.pad(router_b, (0, V7X_LANES - ne)).reshape(1, V7X_LANES)
    fg = final_norm_g.reshape(1, d)

    for i in range(depth):
        j = i // n_mixers
        ng1 = norm1_g[i].reshape(1, d)
        if i % n_mixers == 0:
            nh = lru_wa.shape[1]
            w = lru_w_out.shape[1]
            wg = jnp.concatenate([lru_wa[j], lru_wx[j]], axis=-1).astype(BF16)
            bg = jnp.concatenate([lru_ba[j].reshape(nh, 1, w // nh), lru_bx[j].reshape(nh, 1, w // nh)], axis=-1)
            x = _lru_layer(x, mod[i], ng1, lru_w_in[j].astype(BF16), lru_conv_w[j], lru_conv_b[j].reshape(1, w),
                           wg, bg, lru_lambda[j].reshape(1, w), lru_w_out[j].astype(BF16))
        else:
            w_in = ssd_w_in[j]
            wz = w_in[:, :din].astype(BF16)
            wxbc = w_in[:, din:din + cdim].astype(BF16)
            wdt = jnp.pad(w_in[:, din + cdim:], ((0, 0), (0, lane_pad))).astype(BF16)
            dtb = jnp.pad(ssd_dt_bias[j], (0, lane_pad)).reshape(1, V7X_LANES)
            alog = jnp.pad(ssd_a_log[j], (0, lane_pad)).reshape(1, V7X_LANES)
            dexp = jnp.repeat(ssd_d[j], SSD_HEAD_DIM).reshape(1, din)
            x = _ssd_layer(x, mod[i], ng1, wz, wxbc, wdt, ssd_conv_w[j], ssd_conv_b[j].reshape(1, cdim),
                           dtb, alog, dexp, ssd_norm_g[j].reshape(1, din), expand, ssd_w_out[j].astype(BF16))
        x = _moe_layer(x, mod[i], norm2_g[i].reshape(1, d), rw2, rwh, rb,
                       moe_w_gate[i].astype(BF16), moe_w_up[i].astype(BF16), moe_w_down[i].astype(BF16), fg,
                       final_norm=(i == depth - 1))
    return x
```

```python
import functools

import jax
import jax.numpy as jnp
from jax import lax
from jax.experimental import pallas as pl
from jax.experimental.pallas import tpu as pltpu

F32 = jnp.float32
BF16 = jnp.bfloat16

EPS = 1e-6
LRU_C = 8.0
SSD_CHUNK = 128
SSD_HEAD_DIM = 64
SSD_GROUPS = 4
SSD_STATE = 128
N_EXPERT_GROUPS = 4
TOP_K = 2

V7X_LANES = 128
V7X_SUBLANES = 8
V7X_VMEM_LIMIT_BYTES = 56 * 1024 * 1024
V7X_BF16_ROWS = 16

MOE_SUB = 256
ROW_ALIGN = V7X_BF16_ROWS
MOE_CHUNK = 256


def _params(semantics):
    return pltpu.CompilerParams(dimension_semantics=semantics, vmem_limit_bytes=V7X_VMEM_LIMIT_BYTES)


def _resident(shape):
    return pl.BlockSpec(shape, lambda *_: (0,) * len(shape), pipeline_mode=pl.Buffered(1))


def _norm_mod(x, g, shift, scale):
    y = x * lax.rsqrt(jnp.mean(x * x, axis=-1, keepdims=True) + EPS)
    return (y * g) * (1.0 + scale) + shift


def _softplus(z):
    return jnp.maximum(z, 0.0) + jnp.log1p(jnp.exp(-jnp.abs(z)))


def _sigmoid(z):
    return 0.5 + 0.5 * jnp.tanh(0.5 * z)


def _silu(z):
    return z * _sigmoid(z)


def _dot(a, b):
    return jnp.dot(a, b, preferred_element_type=F32)


def _dot_nt(a, b, precision=None):
    return lax.dot_general(a, b, (((1,), (1,)), ((), ())), precision=precision, preferred_element_type=F32)


def _dot_tn(a, b):
    return lax.dot_general(a, b, (((0,), (0,)), ((), ())), preferred_element_type=F32)


def _ada_body(c_ref, w_ref, b_ref, o_ref):
    cond = _silu(c_ref[...]).astype(BF16)
    o_ref[...] = _dot(cond, w_ref[...].astype(BF16)) + b_ref[...]


def _ada_mod(c, ada_w, ada_b, tn=1536):
    depth, d, n = ada_w.shape
    bsz = c.shape[0]
    return pl.pallas_call(
        _ada_body,
        out_shape=jax.ShapeDtypeStruct((depth, bsz, n), F32),
        grid=(depth, n // tn),
        in_specs=[
            pl.BlockSpec((bsz, d), lambda l, j: (0, 0)),
            pl.BlockSpec((None, d, tn), lambda l, j: (l, 0, j)),
            pl.BlockSpec((None, 1, tn), lambda l, j: (l, 0, j)),
        ],
        out_specs=pl.BlockSpec((None, bsz, tn), lambda l, j: (l, 0, j)),
        compiler_params=_params(("parallel", "parallel")),
        name="ada_mod",
    )(c, ada_w, ada_b.reshape(depth, 1, n))


def _lru_body(x_ref, mod_ref, ng_ref, win_ref, cw_ref, cb_ref, wg_ref, bg_ref, lam_ref, wout_ref,
              o_ref, xpad_scr, h_scr, *, tc):
    nb, ts, d = x_ref.shape
    w = xpad_scr.shape[1]
    rc = nb * tc
    nh = wg_ref.shape[0]
    hw = w // nh
    kc = cw_ref.shape[0]
    halo = (kc - 1) * nb

    @pl.when(pl.program_id(0) == 0)
    def _():
        xpad_scr[0:halo, :] = jnp.zeros((halo, w), F32)
        h_scr[...] = jnp.zeros_like(h_scr)

    mod = mod_ref[...]
    ng = ng_ref[...]
    cw = cw_ref[...]
    cb = cb_ref[...]
    sp = _softplus(-lam_ref[...])

    def chunk(ci, h):
        t0 = pl.multiple_of(ci * tc, tc)
        x = x_ref[:, pl.ds(t0, tc), :]
        hn = _norm_mod(x, ng, mod[:, 0:1, :], mod[:, 1:2, :])
        u = _dot(hn.reshape(rc, d).astype(BF16), win_ref[...])
        gate = u[:, :w]
        xpad_scr[halo:halo + rc, :] = jnp.swapaxes(u[:, w:].reshape(nb, tc, w), 0, 1).reshape(rc, w)
        xc = cb
        for k in range(kc):
            xc = xc + cw[k:k + 1, :] * xpad_scr[k * nb:k * nb + rc, :]
        xpad_scr[0:halo, :] = xpad_scr[rc:rc + halo, :]
        a_parts, b_parts = [], []
        for hd in range(nh):
            cols = slice(hd * hw, (hd + 1) * hw)
            xh = xc[:, cols]
            ri = _dot(xh.astype(BF16), wg_ref[hd]) + bg_ref[hd]
            r = _sigmoid(ri[:, :hw])
            i = _sigmoid(ri[:, hw:])
            log_a = (-LRU_C) * r * sp[:, cols]
            t = jnp.tanh(log_a)
            a_parts.append(jnp.exp(log_a))
            b_parts.append(jnp.sqrt(-2.0 * t / (1.0 - t)) * (i * xh))
        a = jnp.concatenate(a_parts, axis=1)
        b = jnp.concatenate(b_parts, axis=1)
        hs = []
        for ti in range(tc):
            h = a[ti * nb:(ti + 1) * nb, :] * h + b[ti * nb:(ti + 1) * nb, :]
            hs.append(h)
        hs = jnp.swapaxes(jnp.concatenate(hs, axis=0).reshape(tc, nb, w), 0, 1).reshape(rc, w)
        y = (hs * jax.nn.gelu(gate, approximate=True)).astype(BF16)
        out = _dot(y, wout_ref[...]).reshape(nb, tc, d)
        o_ref[:, pl.ds(t0, tc), :] = x + mod[:, 2:3, :] * out
        return h

    h_scr[...] = lax.fori_loop(0, ts // tc, chunk, h_scr[...])


def _lru_layer(x, mod, ng, win, cw, cb, wg, bg, lam, wout, *, ts=128, tc=32):
    bsz, s, d = x.shape
    w = wout.shape[0]
    kc = cw.shape[0]
    return pl.pallas_call(
        functools.partial(_lru_body, tc=tc),
        out_shape=jax.ShapeDtypeStruct(x.shape, F32),
        grid=(s // ts,),
        in_specs=[
            pl.BlockSpec((bsz, ts, d), lambda j: (0, j, 0)),
            _resident(mod.shape), _resident(ng.shape), _resident(win.shape), _resident(cw.shape),
            _resident(cb.shape), _resident(wg.shape), _resident(bg.shape), _resident(lam.shape),
            _resident(wout.shape),
        ],
        out_specs=pl.BlockSpec((bsz, ts, d), lambda j: (0, j, 0)),
        scratch_shapes=[
            pltpu.VMEM(((kc - 1 + tc) * bsz, w), F32),
            pltpu.VMEM((bsz, w), F32),
        ],
        compiler_params=_params(("arbitrary",)),
        name="lru_layer",
    )(x, mod, ng, win, cw, cb, wg, bg, lam, wout)


def _ssd_body(x_ref, mod_ref, ng_ref, wz_ref, wxbc_ref, wdt_ref, cw_ref, cb_ref, dtb_ref, alog_ref,
              dexp_ref, nrm_ref, expand_ref, wout_ref,
              o_ref, z_scr, xpad_scr, act_scr, state_scr):
    rows, d = x_ref.shape
    din = z_scr.shape[1]
    cdim = act_scr.shape[1]
    kc = cw_ref.shape[0]
    halo = V7X_SUBLANES
    L, P, G, N = SSD_CHUNK, SSD_HEAD_DIM, SSD_GROUPS, SSD_STATE
    gw = din // G
    hpg = gw // P

    @pl.when(pl.program_id(1) == 0)
    def _():
        xpad_scr[0:halo, :] = jnp.zeros((halo, cdim), F32)
        state_scr[...] = jnp.zeros_like(state_scr)

    mod = mod_ref[...]
    ng = ng_ref[...]
    cw = cw_ref[...]
    cb = cb_ref[...]
    neg_a = -jnp.exp(alog_ref[...])
    dtb = dtb_ref[...]
    dexp = dexp_ref[...]
    nrm = nrm_ref[...]
    expand = expand_ref[...]
    row_i = lax.broadcasted_iota(jnp.int32, (L, L), 0)
    col_i = lax.broadcasted_iota(jnp.int32, (L, L), 1)
    causal = row_i >= col_i
    tril = causal.astype(F32)
    lane_lo = lax.broadcasted_iota(jnp.int32, (L, 2 * P), 1) < P

    def ssd_chunk(ci, carry):
        r0 = pl.multiple_of(ci * L, L)
        x = x_ref[pl.ds(r0, L), :]
        hn = _norm_mod(x, ng, mod[0:1, :], mod[1:2, :]).astype(BF16)
        z_scr[...] = _dot(hn, wz_ref[...])
        xpad_scr[halo:halo + L, :] = _dot(hn, wxbc_ref[...])
        dt_pre = _dot(hn, wdt_ref[...])
        win = xpad_scr[...]
        acc = cb + cw[kc - 1:kc, :] * win[halo:halo + L, :]
        for k in range(kc - 1):
            acc = acc + cw[k:k + 1, :] * pltpu.roll(win, kc - 1 - k, axis=0)[halo:halo + L, :]
        act_scr[...] = _silu(acc)
        xpad_scr[0:halo, :] = xpad_scr[L:L + halo, :]
        dt = _softplus(dt_pre + dtb)
        da = dt * neg_a
        acs = jnp.dot(tril, da, precision=lax.Precision.HIGHEST, preferred_element_type=F32)
        acs_t = acs.T
        dt_t = dt.T
        last = acs[L - 1:L, :]
        e_decay = jnp.exp(acs)
        w_state = dt * jnp.exp(last - acs)
        ex = _dot(jnp.concatenate([e_decay, w_state], axis=0).astype(BF16), expand)
        e_decay_x = ex[:L]
        w_state_x = ex[L:]
        xs = act_scr[:, 0:din]
        xs_b = xs.astype(BF16)
        xd_b = (xs * w_state_x).astype(BF16)
        ys = []
        for g in range(G):
            bg_ = act_scr[:, din + g * N: din + (g + 1) * N].astype(BF16)
            cg_ = act_scr[:, din + (G + g) * N: din + (G + g + 1) * N].astype(BF16)
            cb_ = _dot_nt(cg_, bg_)
            gcols = slice(g * gw, (g + 1) * gw)
            parts = []
            for hp in range(hpg // 2):
                ms = []
                for hh in range(2):
                    h = g * hpg + 2 * hp + hh
                    seg = acs[:, h:h + 1] - acs_t[h:h + 1, :]
                    lmat = jnp.where(causal, jnp.exp(seg), 0.0)
                    ms.append((cb_ * lmat * dt_t[h:h + 1, :]).astype(BF16))
                c0 = g * gw + hp * 2 * P
                xp = xs_b[:, c0:c0 + 2 * P]
                zero = jnp.zeros_like(xp)
                rhs = jnp.concatenate([jnp.where(lane_lo, xp, zero), jnp.where(lane_lo, zero, xp)], axis=0)
                parts.append(_dot(jnp.concatenate(ms, axis=1), rhs))
            y_diag = jnp.concatenate(parts, axis=1)
            st_prev = state_scr[:, gcols]
            y_off = _dot(cg_, st_prev.astype(BF16)) * e_decay_x[:, gcols]
            st_new = _dot_tn(bg_, xd_b[:, gcols])
            state_scr[:, gcols] = st_prev * e_decay_x[L - 1:L, gcols] + st_new
            yg = y_diag + y_off + dexp[:, gcols] * xs[:, gcols]
            yg = yg * _silu(z_scr[:, gcols])
            yg = yg * lax.rsqrt(jnp.mean(yg * yg, axis=-1, keepdims=True) + EPS)
            ys.append((yg * nrm[:, gcols]).astype(BF16))
        out = _dot(jnp.concatenate(ys, axis=1), wout_ref[...])
        o_ref[pl.ds(r0, L), :] = x + mod[2:3, :] * out
        return carry

    lax.fori_loop(0, rows // L, ssd_chunk, 0)


def _ssd_layer(x, mod, ng, wz, wxbc, wdt, cw, cb, dtb, alog, dexp, nrm, expand, wout, *, rows=1024):
    bsz, s, d = x.shape
    din = wz.shape[1]
    cdim = wxbc.shape[1]
    res = [ng, wz, wxbc, wdt, cw, cb, dtb, alog, dexp, nrm, expand, wout]
    return pl.pallas_call(
        _ssd_body,
        out_shape=jax.ShapeDtypeStruct(x.shape, F32),
        grid=(bsz, s // rows),
        in_specs=[
            pl.BlockSpec((None, rows, d), lambda b, j: (b, j, 0)),
            pl.BlockSpec((None, 6, d), lambda b, j: (b, 0, 0)),
        ] + [_resident(a.shape) for a in res],
        out_specs=pl.BlockSpec((None, rows, d), lambda b, j: (b, j, 0)),
        scratch_shapes=[
            pltpu.VMEM((SSD_CHUNK, din), F32),
            pltpu.VMEM((V7X_SUBLANES + SSD_CHUNK, cdim), F32),
            pltpu.VMEM((SSD_CHUNK, cdim), F32),
            pltpu.VMEM((SSD_STATE, din), F32),
        ],
        compiler_params=_params(("parallel", "arbitrary")),
        name="ssd_layer",
    )(x, mod, *res)


def _route(logits_t, n_groups):
    ne, tm = logits_t.shape
    per = ne // n_groups
    ex = jnp.exp(logits_t - jnp.max(logits_t, axis=0, keepdims=True))
    probs = ex / jnp.sum(ex, axis=0, keepdims=True)
    p3 = probs.reshape(n_groups, per, tm)
    io = lax.broadcasted_iota(jnp.int32, p3.shape, 1)
    m1 = jnp.max(p3, axis=1, keepdims=True)
    sel1 = io == jnp.min(jnp.where(p3 == m1, io, per), axis=1, keepdims=True)
    rest = jnp.where(sel1, -1.0, p3)
    m2 = jnp.max(rest, axis=1, keepdims=True)
    sel2 = io == jnp.min(jnp.where(rest == m2, io, per), axis=1, keepdims=True)
    score = m1 + m2
    gio = lax.broadcasted_iota(jnp.int32, score.shape, 0)
    best = jnp.min(jnp.where(score == jnp.max(score, axis=0, keepdims=True), gio, n_groups),
                   axis=0, keepdims=True)
    top = jnp.where(sel1, m1, 0.0) + jnp.where(sel2, m2, 0.0)
    in_best = gio == best
    gates = jnp.sum(jnp.where(in_best, top / score, 0.0), axis=0)
    return gates, in_best.astype(F32).reshape(n_groups, tm)


def _moe_body(x_ref, mod_ref, ng_ref, rw2_ref, rwh_ref, rb_ref, wg_ref, wu_ref, wd_ref, fg_ref,
              o_ref, reg_scr, greg_scr, perm_scr, srt_scr, gsrt_scr, tab_smem, cnt_smem, *, final_norm):
    grp = pl.program_id(2)
    tm, d = x_ref.shape
    per = wg_ref.shape[0]
    ngroups = N_EXPERT_GROUPS
    nsub = tm // MOE_SUB
    sp = perm_scr.shape[1]
    mod = mod_ref[...]

    first = (pl.program_id(0) == 0) & (pl.program_id(1) == 0) & (grp == 0)

    @pl.when(first)
    def _():
        reg_scr[...] = jnp.zeros_like(reg_scr)
        greg_scr[...] = jnp.zeros_like(greg_scr)
        srt_scr[...] = jnp.zeros_like(srt_scr)
        gsrt_scr[...] = jnp.zeros_like(gsrt_scr)

    @pl.when(grp == 0)
    def _():
        for k in range(ngroups):
            cnt_smem[k] = 0
        tok_i = lax.broadcasted_iota(jnp.int32, (MOE_SUB, MOE_SUB), 0)
        tok_j = lax.broadcasted_iota(jnp.int32, (MOE_SUB, MOE_SUB), 1)
        triu = (tok_i <= tok_j).astype(BF16)
        row_i = lax.broadcasted_iota(jnp.int32, (sp, MOE_SUB), 0).astype(F32)
        for s in range(nsub):
            rows = slice(s * MOE_SUB, (s + 1) * MOE_SUB)
            hn = _norm_mod(x_ref[rows, :], ng_ref[...], mod[3:4, :], mod[4:5, :])
            hi = hn.astype(BF16)
            lo = (hn - hi.astype(F32)).astype(BF16)
            l2 = _dot(hi, rw2_ref[...])
            logits = l2[:, :V7X_LANES] + l2[:, V7X_LANES:] + _dot(lo, rwh_ref[...]) + rb_ref[...]
            gates4, member = _route(logits.T[0:ngroups * per, :], ngroups)
            cum = _dot(member.astype(BF16), triu)
            cnt = cum[:, MOE_SUB - 1:MOE_SUB]
            c16 = jnp.floor((cnt + (ROW_ALIGN - 1.0)) * (1.0 / ROW_ALIGN)) * ROW_ALIGN
            bases = [jnp.zeros((1, 1), F32)]
            for k in range(1, ngroups):
                bases.append(bases[-1] + c16[k - 1:k, :])
            base = jnp.concatenate(bases, axis=0)
            pos = jnp.sum(member * (cum - 1.0 + base), axis=0, keepdims=True)
            pb = jnp.where(row_i == pos, 1.0, 0.0).astype(BF16)
            perm_scr[s] = pb
            srt_scr[s, 0:sp, :] = _dot(pb, hi).astype(BF16)
            g_hi = gates4.astype(BF16).astype(F32)
            g_cat = jnp.concatenate(
                [g_hi, gates4 - g_hi, jnp.zeros((V7X_LANES - 2 * per, MOE_SUB), F32)], axis=0).astype(BF16)
            gsrt_scr[s, 0:sp, :] = _dot_nt(pb, g_cat)
            for k in range(ngroups):
                b0 = pl.multiple_of(base[k, 0].astype(jnp.int32), ROW_ALIGN)
                r0 = pl.multiple_of(cnt_smem[k], ROW_ALIGN)
                t0 = (s * ngroups + k) * 2
                tab_smem[t0] = b0
                tab_smem[t0 + 1] = r0
                reg_scr[k, pl.ds(r0, MOE_SUB), :] = srt_scr[s, pl.ds(b0, MOE_SUB), :]
                greg_scr[k, pl.ds(r0, MOE_SUB), :] = gsrt_scr[s, pl.ds(b0, MOE_SUB), :]
                cnt_smem[k] = r0 + c16[k, 0].astype(jnp.int32)

    def experts(off, m):
        h = reg_scr[grp, pl.ds(off, m), :]
        gt = greg_scr[grp, pl.ds(off, m), :]
        acts = []
        for e in range(per):
            gate = gt[:, e:e + 1] + gt[:, per + e:per + e + 1]
            a = _silu(_dot(h, wg_ref[e])) * _dot(h, wu_ref[e]) * gate
            acts.append(a.astype(BF16))
        out = _dot(jnp.concatenate(acts, axis=1), wd_ref[...])
        reg_scr[grp, pl.ds(off, m), :] = out.astype(BF16)

    n_rows = cnt_smem[grp]
    even = tm // ngroups + nsub * ROW_ALIGN

    @pl.when((n_rows > 0) & (n_rows <= even))
    def _():
        experts(0, even)

    @pl.when(n_rows > even)
    def _():
        n_full = n_rows // MOE_CHUNK

        def full_chunk(ci, carry):
            experts(pl.multiple_of(ci * MOE_CHUNK, MOE_CHUNK), MOE_CHUNK)
            return carry

        lax.fori_loop(0, n_full, full_chunk, 0)
        rem = n_rows - n_full * MOE_CHUNK
        tail = pl.multiple_of(n_full * MOE_CHUNK, MOE_CHUNK)

        @pl.when(rem > MOE_CHUNK // 2)
        def _():
            experts(tail, MOE_CHUNK)

        @pl.when((rem > 0) & (rem <= MOE_CHUNK // 2))
        def _():
            experts(tail, MOE_CHUNK // 2)

    @pl.when(grp == ngroups - 1)
    def _():
        for s in range(nsub):
            rows = slice(s * MOE_SUB, (s + 1) * MOE_SUB)
            for k in range(ngroups):
                t0 = (s * ngroups + k) * 2
                b0 = pl.multiple_of(tab_smem[t0], ROW_ALIGN)
                r0 = pl.multiple_of(tab_smem[t0 + 1], ROW_ALIGN)
                srt_scr[s, pl.ds(b0, MOE_SUB), :] = reg_scr[k, pl.ds(r0, MOE_SUB), :]
            out = _dot_tn(perm_scr[s], srt_scr[s, 0:sp, :])
            y = x_ref[rows, :] + mod[5:6, :] * out
            if final_norm:
                y = y * lax.rsqrt(jnp.mean(y * y, axis=-1, keepdims=True) + EPS) * fg_ref[...]
            o_ref[rows, :] = y


def _moe_layer(x, mod, ng, rw2, rwh, rb, wg, wu, wd, fg, *, final_norm, tm=1024):
    bsz, s, d = x.shape
    ne, _, de = wg.shape
    ngroups = N_EXPERT_GROUPS
    per = ne // ngroups
    nsub = tm // MOE_SUB
    sp = MOE_SUB + ngroups * ROW_ALIGN
    cap = tm + nsub * ngroups * ROW_ALIGN + max(MOE_CHUNK, MOE_SUB)
    small = [ng, rw2, rwh, rb]
    return pl.pallas_call(
        functools.partial(_moe_body, final_norm=final_norm),
        out_shape=jax.ShapeDtypeStruct(x.shape, F32),
        grid=(bsz, s // tm, ngroups),
        in_specs=[
            pl.BlockSpec((None, tm, d), lambda b, j, g: (b, j, 0)),
            pl.BlockSpec((None, 6, d), lambda b, j, g: (b, 0, 0)),
        ] + [_resident(a.shape) for a in small] + [
            pl.BlockSpec((per, d, de), lambda b, j, g: (g, 0, 0)),
            pl.BlockSpec((per, d, de), lambda b, j, g: (g, 0, 0)),
            pl.BlockSpec((None, per * de, d), lambda b, j, g: (g, 0, 0)),
            _resident(fg.shape),
        ],
        out_specs=pl.BlockSpec((None, tm, d), lambda b, j, g: (b, j, 0)),
        scratch_shapes=[
            pltpu.VMEM((ngroups, cap, d), BF16),
            pltpu.VMEM((ngroups, cap, V7X_LANES), F32),
            pltpu.VMEM((nsub, sp, MOE_SUB), BF16),
            pltpu.VMEM((nsub, sp + MOE_SUB, d), BF16),
            pltpu.VMEM((nsub, sp + MOE_SUB, V7X_LANES), F32),
            pltpu.SMEM((nsub * ngroups * 2,), jnp.int32),
            pltpu.SMEM((ngroups,), jnp.int32),
        ],
        compiler_params=_params(("arbitrary", "arbitrary", "arbitrary")),
        name="moe_layer",
    )(x, mod, *small, wg, wu, wd.reshape(ngroups, per * de, d), fg)


def kernel(x, c, ada_w, ada_b, norm1_g, norm2_g, lru_w_in, lru_conv_w, lru_conv_b, lru_wa, lru_ba, lru_wx, lru_bx, lru_lambda, lru_w_out, ssd_w_in, ssd_conv_w, ssd_conv_b, ssd_dt_bias, ssd_a_log, ssd_d, ssd_norm_g, ssd_w_out, router_w, router_b, moe_w_gate, moe_w_up, moe_w_down, final_norm_g):
    bsz, s, d = x.shape
    depth = ada_w.shape[0]
    n_mixers = 2
    mod = _ada_mod(c, ada_w, ada_b).reshape(depth, bsz, 6, d)

    heads = ssd_dt_bias.shape[1]
    din = heads * SSD_HEAD_DIM
    cdim = din + 2 * SSD_GROUPS * SSD_STATE
    lane_pad = V7X_LANES - heads
    expand = (jnp.arange(V7X_LANES)[:, None] == (jnp.arange(din) // SSD_HEAD_DIM)[None, :]).astype(BF16)

    ne = router_w.shape[1]
    rw = jnp.pad(router_w, ((0, 0), (0, V7X_LANES - ne)))
    rwh = rw.astype(BF16)
    rw2 = jnp.concatenate([rwh, (rw - rwh.astype(F32)).astype(BF16)], axis=1)
    rb = jnp.pad(router_b, (0, V7X_LANES - ne)).reshape(1, V7X_LANES)
    fg = final_norm_g.reshape(1, d)

    for i in range(depth):
        j = i // n_mixers
        ng1 = norm1_g[i].reshape(1, d)
        if i % n_mixers == 0:
            nh = lru_wa.shape[1]
            w = lru_w_out.shape[1]
            wg = jnp.concatenate([lru_wa[j], lru_wx[j]], axis=-1).astype(BF16)
            bg = jnp.concatenate([lru_ba[j].reshape(nh, 1, w // nh), lru_bx[j].reshape(nh, 1, w // nh)], axis=-1)
            x = _lru_layer(x, mod[i], ng1, lru_w_in[j].astype(BF16), lru_conv_w[j], lru_conv_b[j].reshape(1, w),
                           wg, bg, lru_lambda[j].reshape(1, w), lru_w_out[j].astype(BF16))
        else:
            w_in = ssd_w_in[j]
            wz = w_in[:, :din].astype(BF16)
            wxbc = w_in[:, din:din + cdim].astype(BF16)
            wdt = jnp.pad(w_in[:, din + cdim:], ((0, 0), (0, lane_pad))).astype(BF16)
            dtb = jnp.pad(ssd_dt_bias[j], (0, lane_pad)).reshape(1, V7X_LANES)
            alog = jnp.pad(ssd_a_log[j], (0, lane_pad)).reshape(1, V7X_LANES)
            dexp = jnp.repeat(ssd_d[j], SSD_HEAD_DIM).reshape(1, din)
            x = _ssd_layer(x, mod[i], ng1, wz, wxbc, wdt, ssd_conv_w[j], ssd_conv_b[j].reshape(1, cdim),
                           dtb, alog, dexp, ssd_norm_g[j].reshape(1, din), expand, ssd_w_out[j].astype(BF16))
        x = _moe_layer(x, mod[i], norm2_g[i].reshape(1, d), rw2, rwh, rb,
                       moe_w_gate[i].astype(BF16), moe_w_up[i].astype(BF16), moe_w_down[i].astype(BF16), fg,
                       final_norm=(i == depth - 1))
    return x
```

```python
import functools

import jax
import jax.numpy as jnp
from jax import lax
from jax.experimental import pallas as pl
from jax.experimental.pallas import tpu as pltpu

F32 = jnp.float32
BF16 = jnp.bfloat16

EPS = 1e-6
LRU_C = 8.0
SSD_CHUNK = 128
SSD_HEAD_DIM = 64
SSD_GROUPS = 4
SSD_STATE = 128
N_EXPERT_GROUPS = 4
TOP_K = 2

V7X_LANES = 128
V7X_SUBLANES = 8
V7X_VMEM_LIMIT_BYTES = 56 * 1024 * 1024
V7X_BF16_ROWS = 16

MOE_SUB = 256
ROW_ALIGN = V7X_BF16_ROWS
MOE_CHUNK = 256


def _params(semantics):
    return pltpu.CompilerParams(dimension_semantics=semantics, vmem_limit_bytes=V7X_VMEM_LIMIT_BYTES)


def _resident(shape):
    return pl.BlockSpec(shape, lambda *_: (0,) * len(shape), pipeline_mode=pl.Buffered(1))


def _norm_mod(x, g, shift, scale):
    y = x * lax.rsqrt(jnp.mean(x * x, axis=-1, keepdims=True) + EPS)
    return (y * g) * (1.0 + scale) + shift


def _softplus(z):
    return jnp.maximum(z, 0.0) + jnp.log1p(jnp.exp(-jnp.abs(z)))


def _sigmoid(z):
    return 0.5 + 0.5 * jnp.tanh(0.5 * z)


def _silu(z):
    return z * _sigmoid(z)


def _dot(a, b):
    return jnp.dot(a, b, preferred_element_type=F32)


def _dot_nt(a, b, precision=None):
    return lax.dot_general(a, b, (((1,), (1,)), ((), ())), precision=precision, preferred_element_type=F32)


def _dot_tn(a, b):
    return lax.dot_general(a, b, (((0,), (0,)), ((), ())), preferred_element_type=F32)


def _ada_body(c_ref, w_ref, b_ref, o_ref):
    cond = _silu(c_ref[...]).astype(BF16)
    o_ref[...] = _dot(cond, w_ref[...].astype(BF16)) + b_ref[...]


def _ada_mod(c, ada_w, ada_b, tn=1536):
    depth, d, n = ada_w.shape
    bsz = c.shape[0]
    return pl.pallas_call(
        _ada_body,
        out_shape=jax.ShapeDtypeStruct((depth, bsz, n), F32),
        grid=(depth, n // tn),
        in_specs=[
            pl.BlockSpec((bsz, d), lambda l, j: (0, 0)),
            pl.BlockSpec((None, d, tn), lambda l, j: (l, 0, j)),
            pl.BlockSpec((None, 1, tn), lambda l, j: (l, 0, j)),
        ],
        out_specs=pl.BlockSpec((None, bsz, tn), lambda l, j: (l, 0, j)),
        compiler_params=_params(("parallel", "parallel")),
        name="ada_mod",
    )(c, ada_w, ada_b.reshape(depth, 1, n))


def _lru_body(x_ref, mod_ref, ng_ref, win_ref, cw_ref, cb_ref, wg_ref, bg_ref, lam_ref, wout_ref,
              o_ref, xpad_scr, h_scr, *, tc):
    nb, ts, d = x_ref.shape
    w = xpad_scr.shape[1]
    rc = nb * tc
    nh = wg_ref.shape[0]
    hw = w // nh
    kc = cw_ref.shape[0]
    halo = (kc - 1) * nb

    @pl.when(pl.program_id(0) == 0)
    def _():
        xpad_scr[0:halo, :] = jnp.zeros((halo, w), F32)
        h_scr[...] = jnp.zeros_like(h_scr)

    mod = mod_ref[...]
    ng = ng_ref[...]
    cw = cw_ref[...]
    cb = cb_ref[...]
    sp = _softplus(-lam_ref[...])

    def chunk(ci, h):
        t0 = pl.multiple_of(ci * tc, tc)
        x = x_ref[:, pl.ds(t0, tc), :]
        hn = _norm_mod(x, ng, mod[:, 0:1, :], mod[:, 1:2, :])
        u = _dot(hn.reshape(rc, d).astype(BF16), win_ref[...])
        gate = u[:, :w]
        xpad_scr[halo:halo + rc, :] = jnp.swapaxes(u[:, w:].reshape(nb, tc, w), 0, 1).reshape(rc, w)
        xc = cb
        for k in range(kc):
            xc = xc + cw[k:k + 1, :] * xpad_scr[k * nb:k * nb + rc, :]
        xpad_scr[0:halo, :] = xpad_scr[rc:rc + halo, :]
        a_parts, b_parts = [], []
        for hd in range(nh):
            cols = slice(hd * hw, (hd + 1) * hw)
            xh = xc[:, cols]
            ri = _dot(xh.astype(BF16), wg_ref[hd]) + bg_ref[hd]
            r = _sigmoid(ri[:, :hw])
            i = _sigmoid(ri[:, hw:])
            log_a = (-LRU_C) * r * sp[:, cols]
            t = jnp.tanh(log_a)
            a_parts.append(jnp.exp(log_a))
            b_parts.append(jnp.sqrt(-2.0 * t / (1.0 - t)) * (i * xh))
        a = jnp.concatenate(a_parts, axis=1)
        b = jnp.concatenate(b_parts, axis=1)
        hs = []
        for ti in range(tc):
            h = a[ti * nb:(ti + 1) * nb, :] * h + b[ti * nb:(ti + 1) * nb, :]
            hs.append(h)
        hs = jnp.swapaxes(jnp.concatenate(hs, axis=0).reshape(tc, nb, w), 0, 1).reshape(rc, w)
        y = (hs * jax.nn.gelu(gate, approximate=True)).astype(BF16)
        out = _dot(y, wout_ref[...]).reshape(nb, tc, d)
        o_ref[:, pl.ds(t0, tc), :] = x + mod[:, 2:3, :] * out
        return h

    h_scr[...] = lax.fori_loop(0, ts // tc, chunk, h_scr[...])


def _lru_layer(x, mod, ng, win, cw, cb, wg, bg, lam, wout, *, ts=128, tc=64):
    bsz, s, d = x.shape
    w = wout.shape[0]
    kc = cw.shape[0]
    return pl.pallas_call(
        functools.partial(_lru_body, tc=tc),
        out_shape=jax.ShapeDtypeStruct(x.shape, F32),
        grid=(s // ts,),
        in_specs=[
            pl.BlockSpec((bsz, ts, d), lambda j: (0, j, 0)),
            _resident(mod.shape), _resident(ng.shape), _resident(win.shape), _resident(cw.shape),
            _resident(cb.shape), _resident(wg.shape), _resident(bg.shape), _resident(lam.shape),
            _resident(wout.shape),
        ],
        out_specs=pl.BlockSpec((bsz, ts, d), lambda j: (0, j, 0)),
        scratch_shapes=[
            pltpu.VMEM(((kc - 1 + tc) * bsz, w), F32),
            pltpu.VMEM((bsz, w), F32),
        ],
        compiler_params=_params(("arbitrary",)),
        name="lru_layer",
    )(x, mod, ng, win, cw, cb, wg, bg, lam, wout)


def _ssd_body(x_ref, mod_ref, ng_ref, wz_ref, wxbc_ref, wdt_ref, cw_ref, cb_ref, dtb_ref, alog_ref,
              dexp_ref, nrm_ref, expand_ref, wout_ref,
              o_ref, z_scr, xpad_scr, act_scr, dt_scr, y_scr, state_scr):
    rows, d = x_ref.shape
    blk = z_scr.shape[0]
    din = z_scr.shape[1]
    cdim = act_scr.shape[1]
    kc = cw_ref.shape[0]
    halo = V7X_SUBLANES
    L, P, G, N = SSD_CHUNK, SSD_HEAD_DIM, SSD_GROUPS, SSD_STATE
    gw = din // G
    hpg = gw // P

    @pl.when(pl.program_id(1) == 0)
    def _():
        xpad_scr[0:halo, :] = jnp.zeros((halo, cdim), F32)
        state_scr[...] = jnp.zeros_like(state_scr)

    mod = mod_ref[...]
    ng = ng_ref[...]
    cw = cw_ref[...]
    cb = cb_ref[...]
    neg_a = -jnp.exp(alog_ref[...])
    dtb = dtb_ref[...]
    dexp = dexp_ref[...]
    nrm = nrm_ref[...]
    expand = expand_ref[...]
    row_i = lax.broadcasted_iota(jnp.int32, (L, L), 0)
    col_i = lax.broadcasted_iota(jnp.int32, (L, L), 1)
    causal = row_i >= col_i
    tril = causal.astype(F32)
    lane_lo = lax.broadcasted_iota(jnp.int32, (L, 2 * P), 1) < P

    def ssd_core(rows_c):
        dt = _softplus(dt_scr[rows_c, :] + dtb)
        da = dt * neg_a
        acs = jnp.dot(tril, da, precision=lax.Precision.HIGHEST, preferred_element_type=F32)
        acs_t = acs.T
        dt_t = dt.T
        last = acs[L - 1:L, :]
        e_decay = jnp.exp(acs)
        w_state = dt * jnp.exp(last - acs)
        ex = _dot(jnp.concatenate([e_decay, w_state], axis=0).astype(BF16), expand)
        e_decay_x = ex[:L]
        w_state_x = ex[L:]
        xs = act_scr[rows_c, 0:din]
        xs_b = xs.astype(BF16)
        xd_b = (xs * w_state_x).astype(BF16)
        for g in range(G):
            bg_ = act_scr[rows_c, din + g * N: din + (g + 1) * N].astype(BF16)
            cg_ = act_scr[rows_c, din + (G + g) * N: din + (G + g + 1) * N].astype(BF16)
            cb_ = _dot_nt(cg_, bg_)
            gcols = slice(g * gw, (g + 1) * gw)
            parts = []
            for hp in range(hpg // 2):
                ms = []
                for hh in range(2):
                    h = g * hpg + 2 * hp + hh
                    seg = acs[:, h:h + 1] - acs_t[h:h + 1, :]
                    lmat = jnp.where(causal, jnp.exp(seg), 0.0)
                    ms.append((cb_ * lmat * dt_t[h:h + 1, :]).astype(BF16))
                c0 = g * gw + hp * 2 * P
                xp = xs_b[:, c0:c0 + 2 * P]
                zero = jnp.zeros_like(xp)
                rhs = jnp.concatenate([jnp.where(lane_lo, xp, zero), jnp.where(lane_lo, zero, xp)], axis=0)
                parts.append(_dot(jnp.concatenate(ms, axis=1), rhs))
            y_diag = jnp.concatenate(parts, axis=1)
            st_prev = state_scr[:, gcols]
            y_off = _dot(cg_, st_prev.astype(BF16)) * e_decay_x[:, gcols]
            st_new = _dot_tn(bg_, xd_b[:, gcols])
            state_scr[:, gcols] = st_prev * e_decay_x[L - 1:L, gcols] + st_new
            yg = y_diag + y_off + dexp[:, gcols] * xs[:, gcols]
            yg = yg * _silu(z_scr[rows_c, gcols])
            yg = yg * lax.rsqrt(jnp.mean(yg * yg, axis=-1, keepdims=True) + EPS)
            y_scr[rows_c, gcols] = (yg * nrm[:, gcols]).astype(BF16)

    def ssd_block(bi, carry):
        r0 = pl.multiple_of(bi * blk, blk)
        x = x_ref[pl.ds(r0, blk), :]
        hn = _norm_mod(x, ng, mod[0:1, :], mod[1:2, :]).astype(BF16)
        z_scr[...] = _dot(hn, wz_ref[...])
        xpad_scr[halo:halo + blk, :] = _dot(hn, wxbc_ref[...])
        dt_scr[...] = _dot(hn, wdt_ref[...])
        win = xpad_scr[...]
        acc = cb + cw[kc - 1:kc, :] * win[halo:halo + blk, :]
        for k in range(kc - 1):
            acc = acc + cw[k:k + 1, :] * pltpu.roll(win, kc - 1 - k, axis=0)[halo:halo + blk, :]
        act_scr[...] = _silu(acc)
        xpad_scr[0:halo, :] = xpad_scr[blk:blk + halo, :]
        for c in range(blk // L):
            ssd_core(slice(c * L, (c + 1) * L))
        out = _dot(y_scr[...], wout_ref[...])
        o_ref[pl.ds(r0, blk), :] = x + mod[2:3, :] * out
        return carry

    lax.fori_loop(0, rows // blk, ssd_block, 0)


def _ssd_layer(x, mod, ng, wz, wxbc, wdt, cw, cb, dtb, alog, dexp, nrm, expand, wout, *, rows=1024, blk=256):
    bsz, s, d = x.shape
    din = wz.shape[1]
    cdim = wxbc.shape[1]
    res = [ng, wz, wxbc, wdt, cw, cb, dtb, alog, dexp, nrm, expand, wout]
    return pl.pallas_call(
        _ssd_body,
        out_shape=jax.ShapeDtypeStruct(x.shape, F32),
        grid=(bsz, s // rows),
        in_specs=[
            pl.BlockSpec((None, rows, d), lambda b, j: (b, j, 0)),
            pl.BlockSpec((None, 6, d), lambda b, j: (b, 0, 0)),
        ] + [_resident(a.shape) for a in res],
        out_specs=pl.BlockSpec((None, rows, d), lambda b, j: (b, j, 0)),
        scratch_shapes=[
            pltpu.VMEM((blk, din), F32),
            pltpu.VMEM((V7X_SUBLANES + blk, cdim), F32),
            pltpu.VMEM((blk, cdim), F32),
            pltpu.VMEM((blk, V7X_LANES), F32),
            pltpu.VMEM((blk, din), BF16),
            pltpu.VMEM((SSD_STATE, din), F32),
        ],
        compiler_params=_params(("parallel", "arbitrary")),
        name="ssd_layer",
    )(x, mod, *res)


def _route(logits_t, n_groups):
    ne, tm = logits_t.shape
    per = ne // n_groups
    ex = jnp.exp(logits_t - jnp.max(logits_t, axis=0, keepdims=True))
    probs = ex / jnp.sum(ex, axis=0, keepdims=True)
    p3 = probs.reshape(n_groups, per, tm)
    io = lax.broadcasted_iota(jnp.int32, p3.shape, 1)
    m1 = jnp.max(p3, axis=1, keepdims=True)
    sel1 = io == jnp.min(jnp.where(p3 == m1, io, per), axis=1, keepdims=True)
    rest = jnp.where(sel1, -1.0, p3)
    m2 = jnp.max(rest, axis=1, keepdims=True)
    sel2 = io == jnp.min(jnp.where(rest == m2, io, per), axis=1, keepdims=True)
    score = m1 + m2
    gio = lax.broadcasted_iota(jnp.int32, score.shape, 0)
    best = jnp.min(jnp.where(score == jnp.max(score, axis=0, keepdims=True), gio, n_groups),
                   axis=0, keepdims=True)
    top = jnp.where(sel1, m1, 0.0) + jnp.where(sel2, m2, 0.0)
    in_best = gio == best
    gates = jnp.sum(jnp.where(in_best, top / score, 0.0), axis=0)
    return gates, in_best.astype(F32).reshape(n_groups, tm)


def _moe_body(x_ref, mod_ref, ng_ref, rw2_ref, rwh_ref, rb_ref, wg_ref, wu_ref, wd_ref, fg_ref,
              o_ref, reg_scr, greg_scr, perm_scr, srt_scr, gsrt_scr, tab_smem, cnt_smem, *, final_norm):
    grp = pl.program_id(2)
    tm, d = x_ref.shape
    per = wg_ref.shape[0]
    ngroups = N_EXPERT_GROUPS
    nsub = tm // MOE_SUB
    sp = perm_scr.shape[1]
    mod = mod_ref[...]

    first = (pl.program_id(0) == 0) & (pl.program_id(1) == 0) & (grp == 0)

    @pl.when(first)
    def _():
        reg_scr[...] = jnp.zeros_like(reg_scr)
        greg_scr[...] = jnp.zeros_like(greg_scr)
        srt_scr[...] = jnp.zeros_like(srt_scr)
        gsrt_scr[...] = jnp.zeros_like(gsrt_scr)

    @pl.when(grp == 0)
    def _():
        for k in range(ngroups):
            cnt_smem[k] = 0
        tok_i = lax.broadcasted_iota(jnp.int32, (MOE_SUB, MOE_SUB), 0)
        tok_j = lax.broadcasted_iota(jnp.int32, (MOE_SUB, MOE_SUB), 1)
        triu = (tok_i <= tok_j).astype(BF16)
        row_i = lax.broadcasted_iota(jnp.int32, (sp, MOE_SUB), 0).astype(F32)
        for s in range(nsub):
            rows = slice(s * MOE_SUB, (s + 1) * MOE_SUB)
            hn = _norm_mod(x_ref[rows, :], ng_ref[...], mod[3:4, :], mod[4:5, :])
            hi = hn.astype(BF16)
            lo = (hn - hi.astype(F32)).astype(BF16)
            l2 = _dot(hi, rw2_ref[...])
            logits = l2[:, :V7X_LANES] + l2[:, V7X_LANES:] + _dot(lo, rwh_ref[...]) + rb_ref[...]
            gates4, member = _route(logits.T[0:ngroups * per, :], ngroups)
            cum = _dot(member.astype(BF16), triu)
            cnt = cum[:, MOE_SUB - 1:MOE_SUB]
            c16 = jnp.floor((cnt + (ROW_ALIGN - 1.0)) * (1.0 / ROW_ALIGN)) * ROW_ALIGN
            bases = [jnp.zeros((1, 1), F32)]
            for k in range(1, ngroups):
                bases.append(bases[-1] + c16[k - 1:k, :])
            base = jnp.concatenate(bases, axis=0)
            pos = jnp.sum(member * (cum - 1.0 + base), axis=0, keepdims=True)
            pb = jnp.where(row_i == pos, 1.0, 0.0).astype(BF16)
            perm_scr[s] = pb
            srt_scr[s % 2, 0:sp, :] = _dot(pb, hi).astype(BF16)
            g_hi = gates4.astype(BF16).astype(F32)
            g_cat = jnp.concatenate(
                [g_hi, gates4 - g_hi, jnp.zeros((V7X_LANES - 2 * per, MOE_SUB), F32)], axis=0).astype(BF16)
            gsrt_scr[s % 2, 0:sp, :] = _dot_nt(pb, g_cat)
            for k in range(ngroups):
                b0 = pl.multiple_of(base[k, 0].astype(jnp.int32), ROW_ALIGN)
                r0 = pl.multiple_of(cnt_smem[k], ROW_ALIGN)
                t0 = (s * ngroups + k) * 2
                tab_smem[t0] = b0
                tab_smem[t0 + 1] = r0
                reg_scr[k, pl.ds(r0, MOE_SUB), :] = srt_scr[s % 2, pl.ds(b0, MOE_SUB), :]
                greg_scr[k, pl.ds(r0, MOE_SUB), :] = gsrt_scr[s % 2, pl.ds(b0, MOE_SUB), :]
                cnt_smem[k] = r0 + c16[k, 0].astype(jnp.int32)

    def experts(off, m):
        h = reg_scr[grp, pl.ds(off, m), :]
        gt = greg_scr[grp, pl.ds(off, m), :]
        acts = []
        for e in range(per):
            gate = gt[:, e:e + 1] + gt[:, per + e:per + e + 1]
            a = _silu(_dot(h, wg_ref[e])) * _dot(h, wu_ref[e]) * gate
            acts.append(a.astype(BF16))
        out = _dot(jnp.concatenate(acts, axis=1), wd_ref[grp])
        reg_scr[grp, pl.ds(off, m), :] = out.astype(BF16)

    n_rows = cnt_smem[grp]
    even = tm // ngroups + nsub * ROW_ALIGN

    @pl.when((n_rows > 0) & (n_rows <= even))
    def _():
        experts(0, even)

    @pl.when(n_rows > even)
    def _():
        n_full = n_rows // MOE_CHUNK

        def full_chunk(ci, carry):
            experts(pl.multiple_of(ci * MOE_CHUNK, MOE_CHUNK), MOE_CHUNK)
            return carry

        lax.fori_loop(0, n_full, full_chunk, 0)
        rem = n_rows - n_full * MOE_CHUNK
        tail = pl.multiple_of(n_full * MOE_CHUNK, MOE_CHUNK)

        @pl.when(rem > MOE_CHUNK // 2)
        def _():
            experts(tail, MOE_CHUNK)

        @pl.when((rem > 0) & (rem <= MOE_CHUNK // 2))
        def _():
            experts(tail, MOE_CHUNK // 2)

    @pl.when(grp == ngroups - 1)
    def _():
        for s in range(nsub):
            rows = slice(s * MOE_SUB, (s + 1) * MOE_SUB)
            for k in range(ngroups):
                t0 = (s * ngroups + k) * 2
                b0 = pl.multiple_of(tab_smem[t0], ROW_ALIGN)
                r0 = pl.multiple_of(tab_smem[t0 + 1], ROW_ALIGN)
                srt_scr[s % 2, pl.ds(b0, MOE_SUB), :] = reg_scr[k, pl.ds(r0, MOE_SUB), :]
            out = _dot_tn(perm_scr[s], srt_scr[s % 2, 0:sp, :])
            y = x_ref[rows, :] + mod[5:6, :] * out
            if final_norm:
                y = y * lax.rsqrt(jnp.mean(y * y, axis=-1, keepdims=True) + EPS) * fg_ref[...]
            o_ref[rows, :] = y


def _moe_layer(x, mod, ng, rw2, rwh, rb, wg, wu, wd, fg, *, final_norm, tm=1024):
    bsz, s, d = x.shape
    ne, _, de = wg.shape
    ngroups = N_EXPERT_GROUPS
    per = ne // ngroups
    nsub = tm // MOE_SUB
    sp = MOE_SUB + ngroups * ROW_ALIGN
    cap = tm + nsub * ngroups * ROW_ALIGN + max(MOE_CHUNK, MOE_SUB)
    small = [ng, rw2, rwh, rb]
    return pl.pallas_call(
        functools.partial(_moe_body, final_norm=final_norm),
        out_shape=jax.ShapeDtypeStruct(x.shape, F32),
        grid=(bsz, s // tm, ngroups),
        in_specs=[
            pl.BlockSpec((None, tm, d), lambda b, j, g: (b, j, 0)),
            pl.BlockSpec((None, 6, d), lambda b, j, g: (b, 0, 0)),
        ] + [_resident(a.shape) for a in small] + [
            pl.BlockSpec((per, d, de), lambda b, j, g: (g, 0, 0)),
            pl.BlockSpec((per, d, de), lambda b, j, g: (g, 0, 0)),
            _resident((ngroups, per * de, d)),
            _resident(fg.shape),
        ],
        out_specs=pl.BlockSpec((None, tm, d), lambda b, j, g: (b, j, 0)),
        scratch_shapes=[
            pltpu.VMEM((ngroups, cap, d), BF16),
            pltpu.VMEM((ngroups, cap, V7X_LANES), F32),
            pltpu.VMEM((nsub, sp, MOE_SUB), BF16),
            pltpu.VMEM((2, sp + MOE_SUB, d), BF16),
            pltpu.VMEM((2, sp + MOE_SUB, V7X_LANES), F32),
            pltpu.SMEM((nsub * ngroups * 2,), jnp.int32),
            pltpu.SMEM((ngroups,), jnp.int32),
        ],
        compiler_params=_params(("arbitrary", "arbitrary", "arbitrary")),
        name="moe_layer",
    )(x, mod, *small, wg, wu, wd.reshape(ngroups, per * de, d), fg)


def kernel(x, c, ada_w, ada_b, norm1_g, norm2_g, lru_w_in, lru_conv_w, lru_conv_b, lru_wa, lru_ba, lru_wx, lru_bx, lru_lambda, lru_w_out, ssd_w_in, ssd_conv_w, ssd_conv_b, ssd_dt_bias, ssd_a_log, ssd_d, ssd_norm_g, ssd_w_out, router_w, router_b, moe_w_gate, moe_w_up, moe_w_down, final_norm_g):
    bsz, s, d = x.shape
    depth = ada_w.shape[0]
    n_mixers = 2
    mod = _ada_mod(c, ada_w, ada_b).reshape(depth, bsz, 6, d)

    heads = ssd_dt_bias.shape[1]
    din = heads * SSD_HEAD_DIM
    cdim = din + 2 * SSD_GROUPS * SSD_STATE
    lane_pad = V7X_LANES - heads
    expand = (jnp.arange(V7X_LANES)[:, None] == (jnp.arange(din) // SSD_HEAD_DIM)[None, :]).astype(BF16)

    ne = router_w.shape[1]
    rw = jnp.pad(router_w, ((0, 0), (0, V7X_LANES - ne)))
    rwh = rw.astype(BF16)
    rw2 = jnp.concatenate([rwh, (rw - rwh.astype(F32)).astype(BF16)], axis=1)
    rb = jnp.pad(router_b, (0, V7X_LANES - ne)).reshape(1, V7X_LANES)
    fg = final_norm_g.reshape(1, d)

    for i in range(depth):
        j = i // n_mixers
        ng1 = norm1_g[i].reshape(1, d)
        if i % n_mixers == 0:
            nh = lru_wa.shape[1]
            w = lru_w_out.shape[1]
            wg = jnp.concatenate([lru_wa[j], lru_wx[j]], axis=-1).astype(BF16)
            bg = jnp.concatenate([lru_ba[j].reshape(nh, 1, w // nh), lru_bx[j].reshape(nh, 1, w // nh)], axis=-1)
            x = _lru_layer(x, mod[i], ng1, lru_w_in[j].astype(BF16), lru_conv_w[j], lru_conv_b[j].reshape(1, w),
                           wg, bg, lru_lambda[j].reshape(1, w), lru_w_out[j].astype(BF16))
        else:
            w_in = ssd_w_in[j]
            wz = w_in[:, :din].astype(BF16)
            wxbc = w_in[:, din:din + cdim].astype(BF16)
            wdt = jnp.pad(w_in[:, din + cdim:], ((0, 0), (0, lane_pad))).astype(BF16)
            dtb = jnp.pad(ssd_dt_bias[j], (0, lane_pad)).reshape(1, V7X_LANES)
            alog = jnp.pad(ssd_a_log[j], (0, lane_pad)).reshape(1, V7X_LANES)
            dexp = jnp.repeat(ssd_d[j], SSD_HEAD_DIM).reshape(1, din)
            x = _ssd_layer(x, mod[i], ng1, wz, wxbc, wdt, ssd_conv_w[j], ssd_conv_b[j].reshape(1, cdim),
                           dtb, alog, dexp, ssd_norm_g[j].reshape(1, din), expand, ssd_w_out[j].astype(BF16))
        x = _moe_layer(x, mod[i], norm2_g[i].reshape(1, d), rw2, rwh, rb,
                       moe_w_gate[i].astype(BF16), moe_w_up[i].astype(BF16), moe_w_down[i].astype(BF16), fg,
                       final_norm=(i == depth - 1))
    return x
```

```python
import functools

import jax
import jax.numpy as jnp
from jax import lax
from jax.experimental import pallas as pl
from jax.experimental.pallas import tpu as pltpu

F32 = jnp.float32
BF16 = jnp.bfloat16

EPS = 1e-6
LRU_C = 8.0
SSD_CHUNK = 128
SSD_HEAD_DIM = 64
SSD_GROUPS = 4
SSD_STATE = 128
N_EXPERT_GROUPS = 4
TOP_K = 2

V7X_LANES = 128
V7X_SUBLANES = 8
V7X_VMEM_LIMIT_BYTES = 56 * 1024 * 1024
V7X_BF16_ROWS = 16

MOE_SUB = 256
ROW_ALIGN = V7X_BF16_ROWS
MOE_CHUNK = 256


def _params(semantics):
    return pltpu.CompilerParams(dimension_semantics=semantics, vmem_limit_bytes=V7X_VMEM_LIMIT_BYTES)


def _resident(shape, layer=None):
    if layer is None:
        return pl.BlockSpec(shape, lambda *_: (0,) * len(shape), pipeline_mode=pl.Buffered(1))
    return pl.BlockSpec((None,) + tuple(shape[1:]), lambda *_: (layer,) + (0,) * (len(shape) - 1),
                        pipeline_mode=pl.Buffered(1))


def _norm_mod(x, g, shift, scale):
    y = x * lax.rsqrt(jnp.mean(x * x, axis=-1, keepdims=True) + EPS)
    return (y * g) * (1.0 + scale) + shift


def _softplus(z):
    return jnp.maximum(z, 0.0) + jnp.log1p(jnp.exp(-jnp.abs(z)))


def _sigmoid(z):
    return 0.5 + 0.5 * jnp.tanh(0.5 * z)


def _silu(z):
    return z * _sigmoid(z)


def _dot(a, b):
    return jnp.dot(a, b, preferred_element_type=F32)


def _dot_nt(a, b, precision=None):
    return lax.dot_general(a, b, (((1,), (1,)), ((), ())), precision=precision, preferred_element_type=F32)


def _dot_tn(a, b):
    return lax.dot_general(a, b, (((0,), (0,)), ((), ())), preferred_element_type=F32)


def _ada_body(c_ref, w_ref, b_ref, o_ref):
    cond = _silu(c_ref[...]).astype(BF16)
    o_ref[...] = _dot(cond, w_ref[...].astype(BF16)) + b_ref[...]


def _ada_mod(c, ada_w, ada_b, tn=1536):
    depth, d, n = ada_w.shape
    bsz = c.shape[0]
    return pl.pallas_call(
        _ada_body,
        out_shape=jax.ShapeDtypeStruct((depth, bsz, n), F32),
        grid=(depth, n // tn),
        in_specs=[
            pl.BlockSpec((bsz, d), lambda l, j: (0, 0)),
            pl.BlockSpec((None, d, tn), lambda l, j: (l, 0, j)),
            pl.BlockSpec((None, 1, tn), lambda l, j: (l, 0, j)),
        ],
        out_specs=pl.BlockSpec((None, bsz, tn), lambda l, j: (l, 0, j)),
        compiler_params=_params(("parallel", "parallel")),
        name="ada_mod",
    )(c, ada_w, ada_b.reshape(depth, 1, n))


def _lru_body(x_ref, mod_ref, ng_ref, win_ref, cw_ref, cb_ref, wg_ref, bg_ref, lam_ref, wout_ref,
              o_ref, xpad_scr, h_scr, *, tc):
    nb, ts, d = x_ref.shape
    w = xpad_scr.shape[1]
    rc = nb * tc
    nh = wg_ref.shape[0]
    hw = w // nh
    kc = cw_ref.shape[0]
    halo = (kc - 1) * nb

    @pl.when(pl.program_id(0) == 0)
    def _():
        xpad_scr[0:halo, :] = jnp.zeros((halo, w), F32)
        h_scr[...] = jnp.zeros_like(h_scr)

    mod = mod_ref[...]
    ng = ng_ref[...]
    cw = cw_ref[...]
    cb = cb_ref[...]
    sp = _softplus(-lam_ref[...])

    def chunk(ci, h):
        t0 = pl.multiple_of(ci * tc, tc)
        x = x_ref[:, pl.ds(t0, tc), :]
        hn = _norm_mod(x, ng, mod[:, 0:1, :], mod[:, 1:2, :])
        u = _dot(hn.reshape(rc, d).astype(BF16), win_ref[...])
        gate = u[:, :w]
        xpad_scr[halo:halo + rc, :] = jnp.swapaxes(u[:, w:].reshape(nb, tc, w), 0, 1).reshape(rc, w)
        xc = cb
        for k in range(kc):
            xc = xc + cw[k:k + 1, :] * xpad_scr[k * nb:k * nb + rc, :]
        xpad_scr[0:halo, :] = xpad_scr[rc:rc + halo, :]
        a_parts, b_parts = [], []
        for hd in range(nh):
            cols = slice(hd * hw, (hd + 1) * hw)
            xh = xc[:, cols]
            ri = _dot(xh.astype(BF16), wg_ref[hd]) + bg_ref[hd]
            r = _sigmoid(ri[:, :hw])
            i = _sigmoid(ri[:, hw:])
            log_a = (-LRU_C) * r * sp[:, cols]
            t = jnp.tanh(log_a)
            a_parts.append(jnp.exp(log_a))
            b_parts.append(jnp.sqrt(-2.0 * t / (1.0 - t)) * (i * xh))
        a = jnp.concatenate(a_parts, axis=1)
        b = jnp.concatenate(b_parts, axis=1)
        hs = []
        for ti in range(tc):
            h = a[ti * nb:(ti + 1) * nb, :] * h + b[ti * nb:(ti + 1) * nb, :]
            hs.append(h)
        hs = jnp.swapaxes(jnp.concatenate(hs, axis=0).reshape(tc, nb, w), 0, 1).reshape(rc, w)
        y = (hs * jax.nn.gelu(gate, approximate=True)).astype(BF16)
        out = _dot(y, wout_ref[...]).reshape(nb, tc, d)
        o_ref[:, pl.ds(t0, tc), :] = x + mod[:, 2:3, :] * out
        return h

    h_scr[...] = lax.fori_loop(0, ts // tc, chunk, h_scr[...])


def _lru_layer(x, mod, ng, win, cw, cb, wg, bg, lam, wout, *, layer, mixer, ts=128, tc=64):
    bsz, s, d = x.shape
    w = wout.shape[1]
    kc = cw.shape[1]
    return pl.pallas_call(
        functools.partial(_lru_body, tc=tc),
        out_shape=jax.ShapeDtypeStruct(x.shape, F32),
        grid=(s // ts,),
        in_specs=[
            pl.BlockSpec((bsz, ts, d), lambda j: (0, j, 0)),
            _resident(mod.shape, layer), _resident(ng.shape, layer),
        ] + [_resident(a.shape, mixer) for a in (win, cw, cb, wg, bg, lam, wout)],
        out_specs=pl.BlockSpec((bsz, ts, d), lambda j: (0, j, 0)),
        scratch_shapes=[
            pltpu.VMEM(((kc - 1 + tc) * bsz, w), F32),
            pltpu.VMEM((bsz, w), F32),
        ],
        compiler_params=_params(("arbitrary",)),
        name="lru_layer",
    )(x, mod, ng, win, cw, cb, wg, bg, lam, wout)


def _ssd_body(x_ref, mod_ref, ng_ref, win_ref, cw_ref, cb_ref, dtb_ref, alog_ref,
              dexp_ref, nrm_ref, expand_ref, wout_ref,
              o_ref, z_scr, xpad_scr, act_scr, dt_scr, y_scr, state_scr):
    rows, d = x_ref.shape
    blk = z_scr.shape[0]
    din = z_scr.shape[1]
    cdim = act_scr.shape[1]
    kc = cw_ref.shape[0]
    halo = V7X_SUBLANES
    L, P, G, N = SSD_CHUNK, SSD_HEAD_DIM, SSD_GROUPS, SSD_STATE
    gw = din // G
    hpg = gw // P

    @pl.when(pl.program_id(1) == 0)
    def _():
        xpad_scr[0:halo, :] = jnp.zeros((halo, cdim), F32)
        state_scr[...] = jnp.zeros_like(state_scr)

    mod = mod_ref[...]
    ng = ng_ref[...]
    cw = cw_ref[...]
    cb = cb_ref[...]
    neg_a = -jnp.exp(alog_ref[...])
    dtb = dtb_ref[...]
    dexp = dexp_ref[...]
    nrm = nrm_ref[...]
    expand = expand_ref[...]
    row_i = lax.broadcasted_iota(jnp.int32, (L, L), 0)
    col_i = lax.broadcasted_iota(jnp.int32, (L, L), 1)
    causal = row_i >= col_i
    tril = causal.astype(F32)
    lane_lo = lax.broadcasted_iota(jnp.int32, (L, 2 * P), 1) < P

    def ssd_core(rows_c):
        dt = _softplus(dt_scr[rows_c, :] + dtb)
        da = dt * neg_a
        acs = jnp.dot(tril, da, precision=lax.Precision.HIGHEST, preferred_element_type=F32)
        acs_t = acs.T
        dt_t = dt.T
        last = acs[L - 1:L, :]
        e_decay = jnp.exp(acs)
        w_state = dt * jnp.exp(last - acs)
        ex = _dot(jnp.concatenate([e_decay, w_state], axis=0).astype(BF16), expand)
        e_decay_x = ex[:L]
        w_state_x = ex[L:]
        xs = act_scr[rows_c, 0:din]
        xs_b = xs.astype(BF16)
        xd_b = (xs * w_state_x).astype(BF16)
        for g in range(G):
            bg_ = act_scr[rows_c, din + g * N: din + (g + 1) * N].astype(BF16)
            cg_ = act_scr[rows_c, din + (G + g) * N: din + (G + g + 1) * N].astype(BF16)
            cb_ = _dot_nt(cg_, bg_)
            gcols = slice(g * gw, (g + 1) * gw)
            parts = []
            for hp in range(hpg // 2):
                ms = []
                for hh in range(2):
                    h = g * hpg + 2 * hp + hh
                    seg = acs[:, h:h + 1] - acs_t[h:h + 1, :]
                    lmat = jnp.where(causal, jnp.exp(seg), 0.0)
                    ms.append((cb_ * lmat * dt_t[h:h + 1, :]).astype(BF16))
                c0 = g * gw + hp * 2 * P
                xp = xs_b[:, c0:c0 + 2 * P]
                zero = jnp.zeros_like(xp)
                rhs = jnp.concatenate([jnp.where(lane_lo, xp, zero), jnp.where(lane_lo, zero, xp)], axis=0)
                parts.append(_dot(jnp.concatenate(ms, axis=1), rhs))
            y_diag = jnp.concatenate(parts, axis=1)
            st_prev = state_scr[:, gcols]
            y_off = _dot(cg_, st_prev.astype(BF16)) * e_decay_x[:, gcols]
            st_new = _dot_tn(bg_, xd_b[:, gcols])
            state_scr[:, gcols] = st_prev * e_decay_x[L - 1:L, gcols] + st_new
            yg = y_diag + y_off + dexp[:, gcols] * xs[:, gcols]
            yg = yg * _silu(z_scr[rows_c, gcols])
            yg = yg * lax.rsqrt(jnp.mean(yg * yg, axis=-1, keepdims=True) + EPS)
            y_scr[rows_c, gcols] = (yg * nrm[:, gcols]).astype(BF16)

    def ssd_block(bi, carry):
        r0 = pl.multiple_of(bi * blk, blk)
        x = x_ref[pl.ds(r0, blk), :]
        hn = _norm_mod(x, ng, mod[0:1, :], mod[1:2, :]).astype(BF16)
        z_scr[...] = _dot(hn, win_ref[:, 0:din])
        xpad_scr[halo:halo + blk, :] = _dot(hn, win_ref[:, din:din + cdim])
        dt_scr[...] = _dot(hn, win_ref[:, din + cdim:])
        win = xpad_scr[...]
        acc = cb + cw[kc - 1:kc, :] * win[halo:halo + blk, :]
        for k in range(kc - 1):
            acc = acc + cw[k:k + 1, :] * pltpu.roll(win, kc - 1 - k, axis=0)[halo:halo + blk, :]
        act_scr[...] = _silu(acc)
        xpad_scr[0:halo, :] = xpad_scr[blk:blk + halo, :]
        for c in range(blk // L):
            ssd_core(slice(c * L, (c + 1) * L))
        out = _dot(y_scr[...], wout_ref[...])
        o_ref[pl.ds(r0, blk), :] = x + mod[2:3, :] * out
        return carry

    lax.fori_loop(0, rows // blk, ssd_block, 0)


def _ssd_layer(x, mod, ng, win, cw, cb, dtb, alog, dexp, nrm, expand, wout, *, layer, mixer, rows=1024, blk=256):
    bsz, s, d = x.shape
    din = wout.shape[1]
    cdim = cw.shape[2]
    res = [win, cw, cb, dtb, alog, dexp, nrm]
    return pl.pallas_call(
        _ssd_body,
        out_shape=jax.ShapeDtypeStruct(x.shape, F32),
        grid=(bsz, s // rows),
        in_specs=[
            pl.BlockSpec((None, rows, d), lambda b, j: (b, j, 0)),
            pl.BlockSpec((None, None, 6, d), lambda b, j: (layer, b, 0, 0)),
            _resident(ng.shape, layer),
        ] + [_resident(a.shape, mixer) for a in res] + [_resident(expand.shape), _resident(wout.shape, mixer)],
        out_specs=pl.BlockSpec((None, rows, d), lambda b, j: (b, j, 0)),
        scratch_shapes=[
            pltpu.VMEM((blk, din), F32),
            pltpu.VMEM((V7X_SUBLANES + blk, cdim), F32),
            pltpu.VMEM((blk, cdim), F32),
            pltpu.VMEM((blk, V7X_LANES), F32),
            pltpu.VMEM((blk, din), BF16),
            pltpu.VMEM((SSD_STATE, din), F32),
        ],
        compiler_params=_params(("parallel", "arbitrary")),
        name="ssd_layer",
    )(x, mod, ng, *res, expand, wout)


def _route(logits_t, n_groups):
    ne, tm = logits_t.shape
    per = ne // n_groups
    ex = jnp.exp(logits_t - jnp.max(logits_t, axis=0, keepdims=True))
    probs = ex / jnp.sum(ex, axis=0, keepdims=True)
    p3 = probs.reshape(n_groups, per, tm)
    io = lax.broadcasted_iota(jnp.int32, p3.shape, 1)
    m1 = jnp.max(p3, axis=1, keepdims=True)
    sel1 = io == jnp.min(jnp.where(p3 == m1, io, per), axis=1, keepdims=True)
    rest = jnp.where(sel1, -1.0, p3)
    m2 = jnp.max(rest, axis=1, keepdims=True)
    sel2 = io == jnp.min(jnp.where(rest == m2, io, per), axis=1, keepdims=True)
    score = m1 + m2
    gio = lax.broadcasted_iota(jnp.int32, score.shape, 0)
    best = jnp.min(jnp.where(score == jnp.max(score, axis=0, keepdims=True), gio, n_groups),
                   axis=0, keepdims=True)
    top = jnp.where(sel1, m1, 0.0) + jnp.where(sel2, m2, 0.0)
    in_best = gio == best
    gates = jnp.sum(jnp.where(in_best, top / score, 0.0), axis=0)
    return gates, in_best.astype(F32).reshape(n_groups, tm)


def _moe_body(x_ref, mod_ref, ng_ref, rw2_ref, rwh_ref, rb_ref, wg_ref, wu_ref, wd_ref, fg_ref,
              o_ref, reg_scr, greg_scr, perm_scr, srt_scr, gsrt_scr, hi_scr, tab_smem, cnt_smem, *, final_norm):
    grp = pl.program_id(2)
    tm, d = x_ref.shape
    per = wg_ref.shape[0]
    ngroups = N_EXPERT_GROUPS
    nsub = tm // MOE_SUB
    sp = perm_scr.shape[1]
    mod = mod_ref[...]

    first = (pl.program_id(0) == 0) & (pl.program_id(1) == 0) & (grp == 0)

    @pl.when(first)
    def _():
        reg_scr[...] = jnp.zeros_like(reg_scr)
        greg_scr[...] = jnp.zeros_like(greg_scr)
        srt_scr[...] = jnp.zeros_like(srt_scr)
        gsrt_scr[...] = jnp.zeros_like(gsrt_scr)

    @pl.when(grp == 0)
    def _():
        for k in range(ngroups):
            cnt_smem[k] = 0
        tok_i = lax.broadcasted_iota(jnp.int32, (MOE_SUB, MOE_SUB), 0)
        tok_j = lax.broadcasted_iota(jnp.int32, (MOE_SUB, MOE_SUB), 1)
        triu = (tok_i <= tok_j).astype(BF16)
        row_i = lax.broadcasted_iota(jnp.int32, (sp, MOE_SUB), 0).astype(F32)
        hn = _norm_mod(x_ref[...], ng_ref[...], mod[3:4, :], mod[4:5, :])
        hi = hn.astype(BF16)
        lo = (hn - hi.astype(F32)).astype(BF16)
        hi_scr[...] = hi
        l2 = _dot(hi, rw2_ref[...])
        logits = l2[:, :V7X_LANES] + l2[:, V7X_LANES:] + _dot(lo, rwh_ref[...]) + rb_ref[...]
        gates4, member = _route(logits.T[0:ngroups * per, :], ngroups)
        g_hi = gates4.astype(BF16).astype(F32)
        g_cat = jnp.concatenate(
            [g_hi, gates4 - g_hi, jnp.zeros((V7X_LANES - 2 * per, tm), F32)], axis=0).astype(BF16)
        member_b = member.astype(BF16)
        subs = [slice(s * MOE_SUB, (s + 1) * MOE_SUB) for s in range(nsub)]
        cums = [_dot(member_b[:, sl], triu) for sl in subs]
        segments = []
        for s in range(nsub):
            cum = cums[s]
            cnt = cum[:, MOE_SUB - 1:MOE_SUB]
            c16 = jnp.floor((cnt + (ROW_ALIGN - 1.0)) * (1.0 / ROW_ALIGN)) * ROW_ALIGN
            bases = [jnp.zeros((1, 1), F32)]
            for k in range(1, ngroups):
                bases.append(bases[-1] + c16[k - 1:k, :])
            base = jnp.concatenate(bases, axis=0)
            pos = jnp.sum(member[:, subs[s]] * (cum - 1.0 + base), axis=0, keepdims=True)
            perm_scr[s] = jnp.where(row_i == pos, 1.0, 0.0).astype(BF16)
            segments.append((base, c16))
        for s in range(nsub):
            srt_scr[s, 0:sp, :] = _dot(perm_scr[s], hi_scr[subs[s], :]).astype(BF16)
            gsrt_scr[s, 0:sp, :] = _dot_nt(perm_scr[s], g_cat[:, subs[s]])
        for s in range(nsub):
            base, c16 = segments[s]
            for k in range(ngroups):
                b0 = pl.multiple_of(base[k, 0].astype(jnp.int32), ROW_ALIGN)
                r0 = pl.multiple_of(cnt_smem[k], ROW_ALIGN)
                t0 = (s * ngroups + k) * 2
                tab_smem[t0] = b0
                tab_smem[t0 + 1] = r0
                reg_scr[k, pl.ds(r0, MOE_SUB), :] = srt_scr[s, pl.ds(b0, MOE_SUB), :]
                greg_scr[k, pl.ds(r0, MOE_SUB), :] = gsrt_scr[s, pl.ds(b0, MOE_SUB), :]
                cnt_smem[k] = r0 + c16[k, 0].astype(jnp.int32)

    def experts(off, m):
        h = reg_scr[grp, pl.ds(off, m), :]
        gt = greg_scr[grp, pl.ds(off, m), :]
        acts = []
        for e in range(per):
            gate = gt[:, e:e + 1] + gt[:, per + e:per + e + 1]
            a = _silu(_dot(h, wg_ref[e])) * _dot(h, wu_ref[e]) * gate
            acts.append(a.astype(BF16))
        out = _dot(jnp.concatenate(acts, axis=1), wd_ref[...])
        reg_scr[grp, pl.ds(off, m), :] = out.astype(BF16)

    n_rows = cnt_smem[grp]
    even = tm // ngroups + nsub * ROW_ALIGN

    @pl.when((n_rows > 0) & (n_rows <= even))
    def _():
        experts(0, even)

    @pl.when(n_rows > even)
    def _():
        n_full = n_rows // MOE_CHUNK

        def full_chunk(ci, carry):
            experts(pl.multiple_of(ci * MOE_CHUNK, MOE_CHUNK), MOE_CHUNK)
            return carry

        lax.fori_loop(0, n_full, full_chunk, 0)
        rem = n_rows - n_full * MOE_CHUNK
        tail = pl.multiple_of(n_full * MOE_CHUNK, MOE_CHUNK)

        @pl.when(rem > MOE_CHUNK // 2)
        def _():
            experts(tail, MOE_CHUNK)

        @pl.when((rem > 0) & (rem <= MOE_CHUNK // 2))
        def _():
            experts(tail, MOE_CHUNK // 2)

    @pl.when(grp == ngroups - 1)
    def _():
        for s in range(nsub):
            rows = slice(s * MOE_SUB, (s + 1) * MOE_SUB)
            for k in range(ngroups):
                t0 = (s * ngroups + k) * 2
                b0 = pl.multiple_of(tab_smem[t0], ROW_ALIGN)
                r0 = pl.multiple_of(tab_smem[t0 + 1], ROW_ALIGN)
                srt_scr[s, pl.ds(b0, MOE_SUB), :] = reg_scr[k, pl.ds(r0, MOE_SUB), :]
            out = _dot_tn(perm_scr[s], srt_scr[s, 0:sp, :])
            y = x_ref[rows, :] + mod[5:6, :] * out
            if final_norm:
                y = y * lax.rsqrt(jnp.mean(y * y, axis=-1, keepdims=True) + EPS) * fg_ref[...]
            o_ref[rows, :] = y


def _moe_layer(x, mod, ng, rw2, rwh, rb, wg, wu, wd, fg, *, layer, final_norm, tm=1024):
    bsz, s, d = x.shape
    _, ne, _, de = wg.shape
    ngroups = N_EXPERT_GROUPS
    per = ne // ngroups
    nsub = tm // MOE_SUB
    sp = MOE_SUB + ngroups * ROW_ALIGN
    cap = tm + nsub * ngroups * ROW_ALIGN + max(MOE_CHUNK, MOE_SUB)
    small = [rw2, rwh, rb]
    return pl.pallas_call(
        functools.partial(_moe_body, final_norm=final_norm),
        out_shape=jax.ShapeDtypeStruct(x.shape, F32),
        grid=(bsz, s // tm, ngroups),
        in_specs=[
            pl.BlockSpec((None, tm, d), lambda b, j, g: (b, j, 0)),
            pl.BlockSpec((None, None, 6, d), lambda b, j, g: (layer, b, 0, 0)),
            _resident(ng.shape, layer),
        ] + [_resident(a.shape) for a in small] + [
            pl.BlockSpec((None, per, d, de), lambda b, j, g: (layer, g, 0, 0)),
            pl.BlockSpec((None, per, d, de), lambda b, j, g: (layer, g, 0, 0)),
            pl.BlockSpec((None, None, per * de, d), lambda b, j, g: (layer, g, 0, 0)),
            _resident(fg.shape),
        ],
        out_specs=pl.BlockSpec((None, tm, d), lambda b, j, g: (b, j, 0)),
        scratch_shapes=[
            pltpu.VMEM((ngroups, cap, d), BF16),
            pltpu.VMEM((ngroups, cap, V7X_LANES), F32),
            pltpu.VMEM((nsub, sp, MOE_SUB), BF16),
            pltpu.VMEM((nsub, sp + MOE_SUB, d), BF16),
            pltpu.VMEM((nsub, sp + MOE_SUB, V7X_LANES), F32),
            pltpu.VMEM((tm, d), BF16),
            pltpu.SMEM((nsub * ngroups * 2,), jnp.int32),
            pltpu.SMEM((ngroups,), jnp.int32),
        ],
        compiler_params=_params(("arbitrary", "arbitrary", "arbitrary")),
        name="moe_layer",
    )(x, mod, ng, *small, wg, wu, wd.reshape(wd.shape[0], ngroups, per * de, d), fg)


def kernel(x, c, ada_w, ada_b, norm1_g, norm2_g, lru_w_in, lru_conv_w, lru_conv_b, lru_wa, lru_ba, lru_wx, lru_bx, lru_lambda, lru_w_out, ssd_w_in, ssd_conv_w, ssd_conv_b, ssd_dt_bias, ssd_a_log, ssd_d, ssd_norm_g, ssd_w_out, router_w, router_b, moe_w_gate, moe_w_up, moe_w_down, final_norm_g):
    bsz, s, d = x.shape
    depth = ada_w.shape[0]
    n_mixers = 2
    mod = _ada_mod(c, ada_w, ada_b).reshape(depth, bsz, 6, d)
    ng1 = norm1_g.reshape(depth, 1, d)
    ng2 = norm2_g.reshape(depth, 1, d)

    n_lru, nh = lru_wa.shape[:2]
    w = lru_w_out.shape[1]
    lru_args = (
        lru_w_in.astype(BF16), lru_conv_w, lru_conv_b.reshape(n_lru, 1, w),
        jnp.concatenate([lru_wa, lru_wx], axis=-1).astype(BF16),
        jnp.concatenate([lru_ba.reshape(n_lru, nh, 1, w // nh), lru_bx.reshape(n_lru, nh, 1, w // nh)], axis=-1),
        lru_lambda.reshape(n_lru, 1, w), lru_w_out.astype(BF16))

    n_ssd, heads = ssd_dt_bias.shape
    din = heads * SSD_HEAD_DIM
    cdim = din + 2 * SSD_GROUPS * SSD_STATE
    lane_pad = V7X_LANES - heads
    expand = (jnp.arange(V7X_LANES)[:, None] == (jnp.arange(din) // SSD_HEAD_DIM)[None, :]).astype(BF16)
    ssd_args = (
        jnp.pad(ssd_w_in, ((0, 0), (0, 0), (0, lane_pad))).astype(BF16),
        ssd_conv_w, ssd_conv_b.reshape(n_ssd, 1, cdim),
        jnp.pad(ssd_dt_bias, ((0, 0), (0, lane_pad))).reshape(n_ssd, 1, V7X_LANES),
        jnp.pad(ssd_a_log, ((0, 0), (0, lane_pad))).reshape(n_ssd, 1, V7X_LANES),
        jnp.repeat(ssd_d, SSD_HEAD_DIM, axis=1).reshape(n_ssd, 1, din),
        ssd_norm_g.reshape(n_ssd, 1, din), expand, ssd_w_out.astype(BF16))
    moe_args = (moe_w_gate.astype(BF16), moe_w_up.astype(BF16), moe_w_down.astype(BF16))

    ne = router_w.shape[1]
    rw = jnp.pad(router_w, ((0, 0), (0, V7X_LANES - ne)))
    rwh = rw.astype(BF16)
    rw2 = jnp.concatenate([rwh, (rw - rwh.astype(F32)).astype(BF16)], axis=1)
    rb = jnp.pad(router_b, (0, V7X_LANES - ne)).reshape(1, V7X_LANES)
    fg = final_norm_g.reshape(1, d)

    for i in range(depth):
        j = i // n_mixers
        if i % n_mixers == 0:
            x = _lru_layer(x, mod, ng1, *lru_args, layer=i, mixer=j)
        else:
            x = _ssd_layer(x, mod, ng1, *ssd_args, layer=i, mixer=j)
        x = _moe_layer(x, mod, ng2, rw2, rwh, rb, *moe_args, fg, layer=i, final_norm=(i == depth - 1))
    return x
```

```python
import functools

import jax
import jax.numpy as jnp
from jax import lax
from jax.experimental import pallas as pl
from jax.experimental.pallas import tpu as pltpu

F32 = jnp.float32
BF16 = jnp.bfloat16

EPS = 1e-6
LRU_C = 8.0
SSD_CHUNK = 128
SSD_HEAD_DIM = 64
SSD_GROUPS = 4
SSD_STATE = 128
N_EXPERT_GROUPS = 4
TOP_K = 2

V7X_LANES = 128
V7X_SUBLANES = 8
V7X_VMEM_LIMIT_BYTES = 56 * 1024 * 1024
V7X_BF16_ROWS = 16

MOE_SUB = 256
ROW_ALIGN = V7X_BF16_ROWS
MOE_CHUNK = 256


def _params(semantics):
    return pltpu.CompilerParams(dimension_semantics=semantics, vmem_limit_bytes=V7X_VMEM_LIMIT_BYTES)


def _resident(shape, layer=None):
    if layer is None:
        return pl.BlockSpec(shape, lambda *_: (0,) * len(shape), pipeline_mode=pl.Buffered(1))
    return pl.BlockSpec((None,) + tuple(shape[1:]), lambda *_: (layer,) + (0,) * (len(shape) - 1),
                        pipeline_mode=pl.Buffered(1))


def _norm_mod(x, g, shift, scale):
    y = x * lax.rsqrt(jnp.mean(x * x, axis=-1, keepdims=True) + EPS)
    return (y * g) * (1.0 + scale) + shift


def _softplus(z):
    return jnp.maximum(z, 0.0) + jnp.log1p(jnp.exp(-jnp.abs(z)))


def _sigmoid(z):
    return 0.5 + 0.5 * jnp.tanh(0.5 * z)


def _silu(z):
    return z * _sigmoid(z)


def _dot(a, b):
    return jnp.dot(a, b, preferred_element_type=F32)


def _dot_nt(a, b, precision=None):
    return lax.dot_general(a, b, (((1,), (1,)), ((), ())), precision=precision, preferred_element_type=F32)


def _dot_tn(a, b):
    return lax.dot_general(a, b, (((0,), (0,)), ((), ())), preferred_element_type=F32)


def _ada_body(c_ref, w_ref, b_ref, o_ref):
    cond = _silu(c_ref[...]).astype(BF16)
    o_ref[...] = _dot(cond, w_ref[...].astype(BF16)) + b_ref[...]


def _ada_mod(c, ada_w, ada_b, tn=1536):
    depth, d, n = ada_w.shape
    bsz = c.shape[0]
    return pl.pallas_call(
        _ada_body,
        out_shape=jax.ShapeDtypeStruct((depth, bsz, n), F32),
        grid=(depth, n // tn),
        in_specs=[
            pl.BlockSpec((bsz, d), lambda l, j: (0, 0)),
            pl.BlockSpec((None, d, tn), lambda l, j: (l, 0, j)),
            pl.BlockSpec((None, 1, tn), lambda l, j: (l, 0, j)),
        ],
        out_specs=pl.BlockSpec((None, bsz, tn), lambda l, j: (l, 0, j)),
        compiler_params=_params(("parallel", "parallel")),
        name="ada_mod",
    )(c, ada_w, ada_b.reshape(depth, 1, n))


def _lru_body(x_ref, mod_ref, ng_ref, win_ref, cw_ref, cb_ref, wg_ref, bg_ref, lam_ref, wout_ref,
              o_ref, xpad_scr, h_scr, *, tc):
    nb, ts, d = x_ref.shape
    w = xpad_scr.shape[1]
    rc = nb * tc
    nh = wg_ref.shape[0]
    hw = w // nh
    kc = cw_ref.shape[0]
    halo = (kc - 1) * nb

    @pl.when(pl.program_id(0) == 0)
    def _():
        xpad_scr[0:halo, :] = jnp.zeros((halo, w), F32)
        h_scr[...] = jnp.zeros_like(h_scr)

    mod = mod_ref[...]
    ng = ng_ref[...]
    cw = cw_ref[...]
    cb = cb_ref[...]
    sp = _softplus(-lam_ref[...])

    def chunk(ci, h):
        t0 = pl.multiple_of(ci * tc, tc)
        x = x_ref[:, pl.ds(t0, tc), :]
        hn = _norm_mod(x, ng, mod[:, 0:1, :], mod[:, 1:2, :])
        u = _dot(hn.reshape(rc, d).astype(BF16), win_ref[...])
        gate = u[:, :w]
        xpad_scr[halo:halo + rc, :] = jnp.swapaxes(u[:, w:].reshape(nb, tc, w), 0, 1).reshape(rc, w)
        xc = cb
        for k in range(kc):
            xc = xc + cw[k:k + 1, :] * xpad_scr[k * nb:k * nb + rc, :]
        xpad_scr[0:halo, :] = xpad_scr[rc:rc + halo, :]
        a_parts, b_parts = [], []
        for hd in range(nh):
            cols = slice(hd * hw, (hd + 1) * hw)
            xh = xc[:, cols]
            ri = _dot(xh.astype(BF16), wg_ref[hd]) + bg_ref[hd]
            r = _sigmoid(ri[:, :hw])
            i = _sigmoid(ri[:, hw:])
            log_a = (-LRU_C) * r * sp[:, cols]
            t = jnp.tanh(log_a)
            a_parts.append(jnp.exp(log_a))
            b_parts.append(jnp.sqrt(-2.0 * t / (1.0 - t)) * (i * xh))
        a = jnp.concatenate(a_parts, axis=1)
        b = jnp.concatenate(b_parts, axis=1)
        hs = []
        for ti in range(tc):
            h = a[ti * nb:(ti + 1) * nb, :] * h + b[ti * nb:(ti + 1) * nb, :]
            hs.append(h)
        hs = jnp.swapaxes(jnp.concatenate(hs, axis=0).reshape(tc, nb, w), 0, 1).reshape(rc, w)
        y = (hs * jax.nn.gelu(gate, approximate=True)).astype(BF16)
        out = _dot(y, wout_ref[...]).reshape(nb, tc, d)
        o_ref[:, pl.ds(t0, tc), :] = x + mod[:, 2:3, :] * out
        return h

    h_scr[...] = lax.fori_loop(0, ts // tc, chunk, h_scr[...])


def _lru_layer(x, mod, ng, win, cw, cb, wg, bg, lam, wout, *, layer, mixer, ts=128, tc=64):
    bsz, s, d = x.shape
    w = wout.shape[1]
    kc = cw.shape[1]
    return pl.pallas_call(
        functools.partial(_lru_body, tc=tc),
        out_shape=jax.ShapeDtypeStruct(x.shape, F32),
        grid=(s // ts,),
        in_specs=[
            pl.BlockSpec((bsz, ts, d), lambda j: (0, j, 0)),
            _resident(mod.shape, layer), _resident(ng.shape, layer),
        ] + [_resident(a.shape, mixer) for a in (win, cw, cb, wg, bg, lam, wout)],
        out_specs=pl.BlockSpec((bsz, ts, d), lambda j: (0, j, 0)),
        scratch_shapes=[
            pltpu.VMEM(((kc - 1 + tc) * bsz, w), F32),
            pltpu.VMEM((bsz, w), F32),
        ],
        compiler_params=_params(("arbitrary",)),
        name="lru_layer",
    )(x, mod, ng, win, cw, cb, wg, bg, lam, wout)


def _ssd_body(x_ref, mod_ref, ng_ref, win_ref, cw_ref, cb_ref, dtb_ref, alog_ref,
              dexp_ref, nrm_ref, expand_ref, wout_ref,
              o_ref, z_scr, xpad_scr, act_scr, dt_scr, y_scr, state_scr):
    rows, d = x_ref.shape
    blk = z_scr.shape[0]
    din = z_scr.shape[1]
    cdim = act_scr.shape[1]
    kc = cw_ref.shape[0]
    halo = V7X_SUBLANES
    L, P, G, N = SSD_CHUNK, SSD_HEAD_DIM, SSD_GROUPS, SSD_STATE
    gw = din // G
    hpg = gw // P

    @pl.when(pl.program_id(1) == 0)
    def _():
        xpad_scr[0:halo, :] = jnp.zeros((halo, cdim), F32)
        state_scr[...] = jnp.zeros_like(state_scr)

    mod = mod_ref[...]
    ng = ng_ref[...]
    cw = cw_ref[...]
    cb = cb_ref[...]
    neg_a = -jnp.exp(alog_ref[...])
    dtb = dtb_ref[...]
    dexp = dexp_ref[...]
    nrm = nrm_ref[...]
    expand = expand_ref[...]
    row_i = lax.broadcasted_iota(jnp.int32, (L, L), 0)
    col_i = lax.broadcasted_iota(jnp.int32, (L, L), 1)
    causal = row_i >= col_i
    tril = causal.astype(F32)
    lane_lo = lax.broadcasted_iota(jnp.int32, (L, 2 * P), 1) < P

    def ssd_core(rows_c):
        dt = _softplus(dt_scr[rows_c, :] + dtb)
        da = dt * neg_a
        acs = jnp.dot(tril, da, precision=lax.Precision.HIGHEST, preferred_element_type=F32)
        acs_t = acs.T
        dt_t = dt.T
        last = acs[L - 1:L, :]
        e_decay = jnp.exp(acs)
        w_state = dt * jnp.exp(last - acs)
        ex = _dot(jnp.concatenate([e_decay, w_state], axis=0).astype(BF16), expand)
        e_decay_x = ex[:L]
        w_state_x = ex[L:]
        xs = act_scr[rows_c, 0:din]
        xs_b = xs.astype(BF16)
        xd_b = (xs * w_state_x).astype(BF16)
        for g in range(G):
            bg_ = act_scr[rows_c, din + g * N: din + (g + 1) * N].astype(BF16)
            cg_ = act_scr[rows_c, din + (G + g) * N: din + (G + g + 1) * N].astype(BF16)
            cb_ = _dot_nt(cg_, bg_)
            gcols = slice(g * gw, (g + 1) * gw)
            parts = []
            for hp in range(hpg // 2):
                ms = []
                for hh in range(2):
                    h = g * hpg + 2 * hp + hh
                    seg = acs[:, h:h + 1] - acs_t[h:h + 1, :]
                    lmat = jnp.where(causal, jnp.exp(seg), 0.0)
                    ms.append((cb_ * lmat * dt_t[h:h + 1, :]).astype(BF16))
                c0 = g * gw + hp * 2 * P
                xp = xs_b[:, c0:c0 + 2 * P]
                zero = jnp.zeros_like(xp)
                rhs = jnp.concatenate([jnp.where(lane_lo, xp, zero), jnp.where(lane_lo, zero, xp)], axis=0)
                parts.append(_dot(jnp.concatenate(ms, axis=1), rhs))
            y_diag = jnp.concatenate(parts, axis=1)
            st_prev = state_scr[:, gcols]
            y_off = _dot(cg_, st_prev.astype(BF16)) * e_decay_x[:, gcols]
            st_new = _dot_tn(bg_, xd_b[:, gcols])
            state_scr[:, gcols] = st_prev * e_decay_x[L - 1:L, gcols] + st_new
            yg = y_diag + y_off + dexp[:, gcols] * xs[:, gcols]
            yg = yg * _silu(z_scr[rows_c, gcols])
            yg = yg * lax.rsqrt(jnp.mean(yg * yg, axis=-1, keepdims=True) + EPS)
            y_scr[rows_c, gcols] = (yg * nrm[:, gcols]).astype(BF16)

    def ssd_block(bi, carry):
        r0 = pl.multiple_of(bi * blk, blk)
        x = x_ref[pl.ds(r0, blk), :]
        hn = _norm_mod(x, ng, mod[0:1, :], mod[1:2, :]).astype(BF16)
        z_scr[...] = _dot(hn, win_ref[:, 0:din])
        xpad_scr[halo:halo + blk, :] = _dot(hn, win_ref[:, din:din + cdim])
        dt_scr[...] = _dot(hn, win_ref[:, din + cdim:])
        win = xpad_scr[...]
        acc = cb + cw[kc - 1:kc, :] * win[halo:halo + blk, :]
        for k in range(kc - 1):
            acc = acc + cw[k:k + 1, :] * pltpu.roll(win, kc - 1 - k, axis=0)[halo:halo + blk, :]
        act_scr[...] = _silu(acc)
        xpad_scr[0:halo, :] = xpad_scr[blk:blk + halo, :]
        for c in range(blk // L):
            ssd_core(slice(c * L, (c + 1) * L))
        out = _dot(y_scr[...], wout_ref[...])
        o_ref[pl.ds(r0, blk), :] = x + mod[2:3, :] * out
        return carry

    lax.fori_loop(0, rows // blk, ssd_block, 0)


def _ssd_layer(x, mod, ng, win, cw, cb, dtb, alog, dexp, nrm, expand, wout, *, layer, mixer, rows=1024, blk=256):
    bsz, s, d = x.shape
    din = wout.shape[1]
    cdim = cw.shape[2]
    res = [win, cw, cb, dtb, alog, dexp, nrm]
    return pl.pallas_call(
        _ssd_body,
        out_shape=jax.ShapeDtypeStruct(x.shape, F32),
        grid=(bsz, s // rows),
        in_specs=[
            pl.BlockSpec((None, rows, d), lambda b, j: (b, j, 0)),
            pl.BlockSpec((None, None, 6, d), lambda b, j: (layer, b, 0, 0)),
            _resident(ng.shape, layer),
        ] + [_resident(a.shape, mixer) for a in res] + [_resident(expand.shape), _resident(wout.shape, mixer)],
        out_specs=pl.BlockSpec((None, rows, d), lambda b, j: (b, j, 0)),
        scratch_shapes=[
            pltpu.VMEM((blk, din), F32),
            pltpu.VMEM((V7X_SUBLANES + blk, cdim), F32),
            pltpu.VMEM((blk, cdim), F32),
            pltpu.VMEM((blk, dtb.shape[2]), F32),
            pltpu.VMEM((blk, din), BF16),
            pltpu.VMEM((SSD_STATE, din), F32),
        ],
        compiler_params=_params(("parallel", "arbitrary")),
        name="ssd_layer",
    )(x, mod, ng, *res, expand, wout)


def _route(logits_t, n_groups):
    ne, tm = logits_t.shape
    per = ne // n_groups
    ex = jnp.exp(logits_t - jnp.max(logits_t, axis=0, keepdims=True))
    probs = ex / jnp.sum(ex, axis=0, keepdims=True)
    p3 = probs.reshape(n_groups, per, tm)
    io = lax.broadcasted_iota(jnp.int32, p3.shape, 1)
    m1 = jnp.max(p3, axis=1, keepdims=True)
    sel1 = io == jnp.min(jnp.where(p3 == m1, io, per), axis=1, keepdims=True)
    rest = jnp.where(sel1, -1.0, p3)
    m2 = jnp.max(rest, axis=1, keepdims=True)
    sel2 = io == jnp.min(jnp.where(rest == m2, io, per), axis=1, keepdims=True)
    score = m1 + m2
    gio = lax.broadcasted_iota(jnp.int32, score.shape, 0)
    best = jnp.min(jnp.where(score == jnp.max(score, axis=0, keepdims=True), gio, n_groups),
                   axis=0, keepdims=True)
    top = jnp.where(sel1, m1, 0.0) + jnp.where(sel2, m2, 0.0)
    in_best = gio == best
    gates = jnp.sum(jnp.where(in_best, top / score, 0.0), axis=0)
    return gates, in_best.astype(F32).reshape(n_groups, tm)


def _moe_body(x_ref, mod_ref, ng_ref, rw2_ref, rwh_ref, rb_ref, wg_ref, wu_ref, wd_ref, fg_ref,
              o_ref, reg_scr, greg_scr, perm_scr, srt_scr, gsrt_scr, hi_scr, tab_smem, cnt_smem, *, final_norm):
    grp = pl.program_id(2)
    tm, d = x_ref.shape
    per = wg_ref.shape[0]
    ngroups = N_EXPERT_GROUPS
    nsub = tm // MOE_SUB
    sp = perm_scr.shape[1]
    mod = mod_ref[...]

    first = (pl.program_id(0) == 0) & (pl.program_id(1) == 0) & (grp == 0)

    @pl.when(first)
    def _():
        reg_scr[...] = jnp.zeros_like(reg_scr)
        greg_scr[...] = jnp.zeros_like(greg_scr)
        srt_scr[...] = jnp.zeros_like(srt_scr)
        gsrt_scr[...] = jnp.zeros_like(gsrt_scr)

    @pl.when(grp == 0)
    def _():
        for k in range(ngroups):
            cnt_smem[k] = 0
        tok_i = lax.broadcasted_iota(jnp.int32, (MOE_SUB, MOE_SUB), 0)
        tok_j = lax.broadcasted_iota(jnp.int32, (MOE_SUB, MOE_SUB), 1)
        triu = (tok_i <= tok_j).astype(BF16)
        row_i = lax.broadcasted_iota(jnp.int32, (sp, MOE_SUB), 0).astype(F32)
        hn = _norm_mod(x_ref[...], ng_ref[...], mod[3:4, :], mod[4:5, :])
        hi = hn.astype(BF16)
        lo = (hn - hi.astype(F32)).astype(BF16)
        hi_scr[...] = hi
        l2 = _dot(hi, rw2_ref[...])
        logits = l2[:, :V7X_LANES] + l2[:, V7X_LANES:] + _dot(lo, rwh_ref[...]) + rb_ref[...]
        gates4, member = _route(logits.T[0:ngroups * per, :], ngroups)
        g_hi = gates4.astype(BF16).astype(F32)
        g_cat = jnp.concatenate(
            [g_hi, gates4 - g_hi, jnp.zeros((V7X_LANES - 2 * per, tm), F32)], axis=0).astype(BF16)
        member_b = member.astype(BF16)
        subs = [slice(s * MOE_SUB, (s + 1) * MOE_SUB) for s in range(nsub)]
        cums = [_dot(member_b[:, sl], triu) for sl in subs]
        segments = []
        for s in range(nsub):
            cum = cums[s]
            cnt = cum[:, MOE_SUB - 1:MOE_SUB]
            c16 = jnp.floor((cnt + (ROW_ALIGN - 1.0)) * (1.0 / ROW_ALIGN)) * ROW_ALIGN
            bases = [jnp.zeros((1, 1), F32)]
            for k in range(1, ngroups):
                bases.append(bases[-1] + c16[k - 1:k, :])
            base = jnp.concatenate(bases, axis=0)
            pos = jnp.sum(member[:, subs[s]] * (cum - 1.0 + base), axis=0, keepdims=True)
            perm_scr[s] = jnp.where(row_i == pos, 1.0, 0.0).astype(BF16)
            segments.append((base, c16))
        for s in range(nsub):
            srt_scr[s, 0:sp, :] = _dot(perm_scr[s], hi_scr[subs[s], :]).astype(BF16)
            gsrt_scr[s, 0:sp, :] = _dot_nt(perm_scr[s], g_cat[:, subs[s]])
        for s in range(nsub):
            base, c16 = segments[s]
            for k in range(ngroups):
                b0 = pl.multiple_of(base[k, 0].astype(jnp.int32), ROW_ALIGN)
                r0 = pl.multiple_of(cnt_smem[k], ROW_ALIGN)
                t0 = (s * ngroups + k) * 2
                tab_smem[t0] = b0
                tab_smem[t0 + 1] = r0
                reg_scr[k, pl.ds(r0, MOE_SUB), :] = srt_scr[s, pl.ds(b0, MOE_SUB), :]
                greg_scr[k, pl.ds(r0, MOE_SUB), :] = gsrt_scr[s, pl.ds(b0, MOE_SUB), :]
                cnt_smem[k] = r0 + c16[k, 0].astype(jnp.int32)

    def experts(off, m):
        h = reg_scr[grp, pl.ds(off, m), :]
        gt = greg_scr[grp, pl.ds(off, m), :]
        acts = []
        for e in range(per):
            gate = gt[:, e:e + 1] + gt[:, per + e:per + e + 1]
            a = _silu(_dot(h, wg_ref[e])) * _dot(h, wu_ref[e]) * gate
            acts.append(a.astype(BF16))
        out = _dot(jnp.concatenate(acts, axis=1), wd_ref[...])
        reg_scr[grp, pl.ds(off, m), :] = out.astype(BF16)

    n_rows = cnt_smem[grp]
    even = tm // ngroups + nsub * ROW_ALIGN

    @pl.when((n_rows > 0) & (n_rows <= even))
    def _():
        experts(0, even)

    @pl.when(n_rows > even)
    def _():
        n_full = n_rows // MOE_CHUNK

        def full_chunk(ci, carry):
            experts(pl.multiple_of(ci * MOE_CHUNK, MOE_CHUNK), MOE_CHUNK)
            return carry

        lax.fori_loop(0, n_full, full_chunk, 0)
        rem = n_rows - n_full * MOE_CHUNK
        tail = pl.multiple_of(n_full * MOE_CHUNK, MOE_CHUNK)

        @pl.when(rem > MOE_CHUNK // 2)
        def _():
            experts(tail, MOE_CHUNK)

        @pl.when((rem > 0) & (rem <= MOE_CHUNK // 2))
        def _():
            experts(tail, MOE_CHUNK // 2)

    @pl.when(grp == ngroups - 1)
    def _():
        for s in range(nsub):
            rows = slice(s * MOE_SUB, (s + 1) * MOE_SUB)
            for k in range(ngroups):
                t0 = (s * ngroups + k) * 2
                b0 = pl.multiple_of(tab_smem[t0], ROW_ALIGN)
                r0 = pl.multiple_of(tab_smem[t0 + 1], ROW_ALIGN)
                srt_scr[s, pl.ds(b0, MOE_SUB), :] = reg_scr[k, pl.ds(r0, MOE_SUB), :]
            out = _dot_tn(perm_scr[s], srt_scr[s, 0:sp, :])
            y = x_ref[rows, :] + mod[5:6, :] * out
            if final_norm:
                y = y * lax.rsqrt(jnp.mean(y * y, axis=-1, keepdims=True) + EPS) * fg_ref[...]
            o_ref[rows, :] = y


def _moe_layer(x, mod, ng, rw2, rwh, rb, wg, wu, wd, fg, *, layer, final_norm, tm=1024):
    bsz, s, d = x.shape
    _, ne, _, de = wg.shape
    ngroups = N_EXPERT_GROUPS
    per = ne // ngroups
    nsub = tm // MOE_SUB
    sp = MOE_SUB + ngroups * ROW_ALIGN
    cap = tm + nsub * ngroups * ROW_ALIGN + max(MOE_CHUNK, MOE_SUB)
    small = [rw2, rwh, rb]
    return pl.pallas_call(
        functools.partial(_moe_body, final_norm=final_norm),
        out_shape=jax.ShapeDtypeStruct(x.shape, F32),
        grid=(bsz, s // tm, ngroups),
        in_specs=[
            pl.BlockSpec((None, tm, d), lambda b, j, g: (b, j, 0)),
            pl.BlockSpec((None, None, 6, d), lambda b, j, g: (layer, b, 0, 0)),
            _resident(ng.shape, layer),
        ] + [_resident(a.shape) for a in small] + [
            pl.BlockSpec((None, per, d, de), lambda b, j, g: (layer, g, 0, 0)),
            pl.BlockSpec((None, per, d, de), lambda b, j, g: (layer, g, 0, 0)),
            pl.BlockSpec((None, None, per * de, d), lambda b, j, g: (layer, g, 0, 0)),
            _resident(fg.shape),
        ],
        out_specs=pl.BlockSpec((None, tm, d), lambda b, j, g: (b, j, 0)),
        scratch_shapes=[
            pltpu.VMEM((ngroups, cap, d), BF16),
            pltpu.VMEM((ngroups, cap, V7X_LANES), F32),
            pltpu.VMEM((nsub, sp, MOE_SUB), BF16),
            pltpu.VMEM((nsub, sp + MOE_SUB, d), BF16),
            pltpu.VMEM((nsub, sp + MOE_SUB, V7X_LANES), F32),
            pltpu.VMEM((tm, d), BF16),
            pltpu.SMEM((nsub * ngroups * 2,), jnp.int32),
            pltpu.SMEM((ngroups,), jnp.int32),
        ],
        compiler_params=_params(("arbitrary", "arbitrary", "arbitrary")),
        name="moe_layer",
    )(x, mod, ng, *small, wg, wu, wd.reshape(wd.shape[0], ngroups, per * de, d), fg)


def kernel(x, c, ada_w, ada_b, norm1_g, norm2_g, lru_w_in, lru_conv_w, lru_conv_b, lru_wa, lru_ba, lru_wx, lru_bx, lru_lambda, lru_w_out, ssd_w_in, ssd_conv_w, ssd_conv_b, ssd_dt_bias, ssd_a_log, ssd_d, ssd_norm_g, ssd_w_out, router_w, router_b, moe_w_gate, moe_w_up, moe_w_down, final_norm_g):
    bsz, s, d = x.shape
    depth = ada_w.shape[0]
    n_mixers = 2
    mod = _ada_mod(c, ada_w, ada_b).reshape(depth, bsz, 6, d)
    ng1 = norm1_g.reshape(depth, 1, d)
    ng2 = norm2_g.reshape(depth, 1, d)

    n_lru, nh = lru_wa.shape[:2]
    w = lru_w_out.shape[1]
    lru_args = (
        lru_w_in.astype(BF16), lru_conv_w, lru_conv_b.reshape(n_lru, 1, w),
        jnp.concatenate([lru_wa, lru_wx], axis=-1).astype(BF16),
        jnp.concatenate([lru_ba.reshape(n_lru, nh, 1, w // nh), lru_bx.reshape(n_lru, nh, 1, w // nh)], axis=-1),
        lru_lambda.reshape(n_lru, 1, w), lru_w_out.astype(BF16))

    n_ssd, heads = ssd_dt_bias.shape
    din = heads * SSD_HEAD_DIM
    cdim = din + 2 * SSD_GROUPS * SSD_STATE
    expand = (jnp.arange(heads)[:, None] == (jnp.arange(din) // SSD_HEAD_DIM)[None, :]).astype(BF16)
    ssd_args = (
        ssd_w_in.astype(BF16), ssd_conv_w, ssd_conv_b.reshape(n_ssd, 1, cdim),
        ssd_dt_bias.reshape(n_ssd, 1, heads), ssd_a_log.reshape(n_ssd, 1, heads),
        jnp.repeat(ssd_d, SSD_HEAD_DIM, axis=1).reshape(n_ssd, 1, din),
        ssd_norm_g.reshape(n_ssd, 1, din), expand, ssd_w_out.astype(BF16))
    moe_args = (moe_w_gate.astype(BF16), moe_w_up.astype(BF16), moe_w_down.astype(BF16))

    ne = router_w.shape[1]
    rw = jnp.pad(router_w, ((0, 0), (0, V7X_LANES - ne)))
    rwh = rw.astype(BF16)
    rw2 = jnp.concatenate([rwh, (rw - rwh.astype(F32)).astype(BF16)], axis=1)
    rb = jnp.pad(router_b, (0, V7X_LANES - ne)).reshape(1, V7X_LANES)
    fg = final_norm_g.reshape(1, d)

    for i in range(depth):
        j = i // n_mixers
        if i % n_mixers == 0:
            x = _lru_layer(x, mod, ng1, *lru_args, layer=i, mixer=j)
        else:
            x = _ssd_layer(x, mod, ng1, *ssd_args, layer=i, mixer=j)
        x = _moe_layer(x, mod, ng2, rw2, rwh, rb, *moe_args, fg, layer=i, final_norm=(i == depth - 1))
    return x
```

```python
import functools

import jax
import jax.numpy as jnp
from jax import lax
from jax.experimental import pallas as pl
from jax.experimental.pallas import tpu as pltpu

F32 = jnp.float32
BF16 = jnp.bfloat16

EPS = 1e-6
LRU_C = 8.0
LOG2_E = 1.4426950408889634
SSD_CHUNK = 128
SSD_HEAD_DIM = 64
SSD_GROUPS = 4
SSD_STATE = 128
N_EXPERT_GROUPS = 4
TOP_K = 2

V7X_LANES = 128
V7X_SUBLANES = 8
V7X_VMEM_LIMIT_BYTES = 56 * 1024 * 1024
V7X_BF16_ROWS = 16

MOE_SUB = 256
ROW_ALIGN = V7X_BF16_ROWS
MOE_CHUNK = 256


def _params(semantics):
    return pltpu.CompilerParams(dimension_semantics=semantics, vmem_limit_bytes=V7X_VMEM_LIMIT_BYTES)


def _resident(shape, layer=None):
    if layer is None:
        return pl.BlockSpec(shape, lambda *_: (0,) * len(shape), pipeline_mode=pl.Buffered(1))
    return pl.BlockSpec((None,) + tuple(shape[1:]), lambda *_: (layer,) + (0,) * (len(shape) - 1),
                        pipeline_mode=pl.Buffered(1))


def _norm_mod(x, g, shift, scale):
    y = x * lax.rsqrt(jnp.mean(x * x, axis=-1, keepdims=True) + EPS)
    return (y * g) * (1.0 + scale) + shift


def _softplus(z):
    return jnp.maximum(z, 0.0) + jnp.log1p(jnp.exp(-jnp.abs(z)))


def _sigmoid(z):
    return 0.5 + 0.5 * jnp.tanh(0.5 * z)


def _silu(z):
    return z * _sigmoid(z)


def _dot(a, b):
    return jnp.dot(a, b, preferred_element_type=F32)


def _dot_nt(a, b, precision=None):
    return lax.dot_general(a, b, (((1,), (1,)), ((), ())), precision=precision, preferred_element_type=F32)


def _dot_tn(a, b):
    return lax.dot_general(a, b, (((0,), (0,)), ((), ())), preferred_element_type=F32)


def _ada_body(c_ref, w_ref, b_ref, o_ref):
    cond = _silu(c_ref[...]).astype(BF16)
    o_ref[...] = _dot(cond, w_ref[...].astype(BF16)) + b_ref[...]


def _ada_mod(c, ada_w, ada_b, tn=1536):
    depth, d, n = ada_w.shape
    bsz = c.shape[0]
    return pl.pallas_call(
        _ada_body,
        out_shape=jax.ShapeDtypeStruct((depth, bsz, n), F32),
        grid=(depth, n // tn),
        in_specs=[
            pl.BlockSpec((bsz, d), lambda l, j: (0, 0)),
            pl.BlockSpec((None, d, tn), lambda l, j: (l, 0, j)),
            pl.BlockSpec((None, 1, tn), lambda l, j: (l, 0, j)),
        ],
        out_specs=pl.BlockSpec((None, bsz, tn), lambda l, j: (l, 0, j)),
        compiler_params=_params(("parallel", "parallel")),
        name="ada_mod",
    )(c, ada_w, ada_b.reshape(depth, 1, n))


def _lru_body(x_ref, mod_ref, ng_ref, win_ref, cw_ref, cb_ref, wg_ref, bg_ref, lam_ref, wout_ref,
              o_ref, xpad_scr, h_scr, *, tc):
    nb, ts, d = x_ref.shape
    w = xpad_scr.shape[1]
    rc = nb * tc
    nh = wg_ref.shape[0]
    hw = w // nh
    kc = cw_ref.shape[0]
    halo = (kc - 1) * nb

    @pl.when(pl.program_id(0) == 0)
    def _():
        xpad_scr[0:halo, :] = jnp.zeros((halo, w), F32)
        h_scr[...] = jnp.zeros_like(h_scr)

    mod = mod_ref[...]
    ng = ng_ref[...]
    cw = cw_ref[...]
    cb = cb_ref[...]
    sp_c = (-LRU_C) * _softplus(-lam_ref[...])
    sp_c2 = sp_c * LOG2_E

    def chunk(ci, h):
        t0 = pl.multiple_of(ci * tc, tc)
        x = x_ref[:, pl.ds(t0, tc), :]
        hn = _norm_mod(x, ng, mod[:, 0:1, :], mod[:, 1:2, :])
        u = _dot(hn.reshape(rc, d).astype(BF16), win_ref[...])
        gate = u[:, :w]
        xpad_scr[halo:halo + rc, :] = jnp.swapaxes(u[:, w:].reshape(nb, tc, w), 0, 1).reshape(rc, w)
        xc = cb
        for k in range(kc):
            xc = xc + cw[k:k + 1, :] * xpad_scr[k * nb:k * nb + rc, :]
        xpad_scr[0:halo, :] = xpad_scr[rc:rc + halo, :]
        a_parts, b_parts = [], []
        for hd in range(nh):
            cols = slice(hd * hw, (hd + 1) * hw)
            xh = xc[:, cols]
            ri = _dot(xh.astype(BF16), wg_ref[hd]) + bg_ref[hd]
            r = 0.5 + 0.5 * jnp.tanh(ri[:, :hw])
            i = 0.5 + 0.5 * jnp.tanh(ri[:, hw:])
            log_a = r * sp_c[:, cols]
            q = -2.0 * jnp.tanh(log_a)
            a_parts.append(jnp.exp2(r * sp_c2[:, cols]))
            coef = jnp.where(q > 0.0, q * lax.rsqrt(q * (1.0 + 0.5 * q)), 0.0)
            b_parts.append(coef * (i * xh))
        a = jnp.concatenate(a_parts, axis=1)
        b = jnp.concatenate(b_parts, axis=1)
        hs = []
        for ti in range(tc):
            h = a[ti * nb:(ti + 1) * nb, :] * h + b[ti * nb:(ti + 1) * nb, :]
            hs.append(h)
        hs = jnp.swapaxes(jnp.concatenate(hs, axis=0).reshape(tc, nb, w), 0, 1).reshape(rc, w)
        y = (hs * jax.nn.gelu(gate, approximate=True)).astype(BF16)
        out = _dot(y, wout_ref[...]).reshape(nb, tc, d)
        o_ref[:, pl.ds(t0, tc), :] = x + mod[:, 2:3, :] * out
        return h

    h_scr[...] = lax.fori_loop(0, ts // tc, chunk, h_scr[...])


def _lru_layer(x, mod, ng, win, cw, cb, wg, bg, lam, wout, *, layer, mixer, ts=128, tc=64):
    bsz, s, d = x.shape
    w = wout.shape[1]
    kc = cw.shape[1]
    return pl.pallas_call(
        functools.partial(_lru_body, tc=tc),
        out_shape=jax.ShapeDtypeStruct(x.shape, F32),
        grid=(s // ts,),
        in_specs=[
            pl.BlockSpec((bsz, ts, d), lambda j: (0, j, 0)),
            _resident(mod.shape, layer), _resident(ng.shape, layer),
        ] + [_resident(a.shape, mixer) for a in (win, cw, cb, wg, bg, lam, wout)],
        out_specs=pl.BlockSpec((bsz, ts, d), lambda j: (0, j, 0)),
        scratch_shapes=[
            pltpu.VMEM(((kc - 1 + tc) * bsz, w), F32),
            pltpu.VMEM((bsz, w), F32),
        ],
        compiler_params=_params(("arbitrary",)),
        name="lru_layer",
    )(x, mod, ng, win, cw, cb, wg, bg, lam, wout)


def _ssd_body(x_ref, mod_ref, ng_ref, win_ref, cw_ref, cb_ref, dtb_ref, alog_ref,
              dexp_ref, nrm_ref, expand_ref, wout_ref,
              o_ref, z_scr, xpad_scr, act_scr, dt_scr, y_scr, state_scr):
    rows, d = x_ref.shape
    blk = z_scr.shape[0]
    din = z_scr.shape[1]
    cdim = act_scr.shape[1]
    kc = cw_ref.shape[0]
    halo = V7X_SUBLANES
    L, P, G, N = SSD_CHUNK, SSD_HEAD_DIM, SSD_GROUPS, SSD_STATE
    gw = din // G
    hpg = gw // P

    @pl.when(pl.program_id(1) == 0)
    def _():
        xpad_scr[0:halo, :] = jnp.zeros((halo, cdim), F32)
        state_scr[...] = jnp.zeros_like(state_scr)

    mod = mod_ref[...]
    ng = ng_ref[...]
    cw = cw_ref[...]
    cb = cb_ref[...]
    neg_a = -jnp.exp(alog_ref[...])
    dtb = dtb_ref[...]
    dexp = dexp_ref[...]
    nrm = nrm_ref[...]
    expand = expand_ref[...]
    row_i = lax.broadcasted_iota(jnp.int32, (L, L), 0)
    col_i = lax.broadcasted_iota(jnp.int32, (L, L), 1)
    causal = row_i >= col_i
    tril = causal.astype(F32)
    lane_lo = lax.broadcasted_iota(jnp.int32, (L, 2 * P), 1) < P

    def ssd_core(rows_c):
        dt = _softplus(dt_scr[rows_c, :] + dtb)
        da = dt * neg_a
        acs = jnp.dot(tril, da, precision=lax.Precision.HIGHEST, preferred_element_type=F32)
        acs_t = acs.T
        dt_t = dt.T
        last = acs[L - 1:L, :]
        e_decay = jnp.exp(acs)
        w_state = dt * jnp.exp(last - acs)
        ex = _dot(jnp.concatenate([e_decay, w_state], axis=0).astype(BF16), expand)
        e_decay_x = ex[:L]
        w_state_x = ex[L:]
        xs = act_scr[rows_c, 0:din]
        xs_b = xs.astype(BF16)
        xd_b = (xs * w_state_x).astype(BF16)
        for g in range(G):
            bg_ = act_scr[rows_c, din + g * N: din + (g + 1) * N].astype(BF16)
            cg_ = act_scr[rows_c, din + (G + g) * N: din + (G + g + 1) * N].astype(BF16)
            cb_ = _dot_nt(cg_, bg_)
            gcols = slice(g * gw, (g + 1) * gw)
            parts = []
            for hp in range(hpg // 2):
                ms = []
                for hh in range(2):
                    h = g * hpg + 2 * hp + hh
                    seg = acs[:, h:h + 1] - acs_t[h:h + 1, :]
                    lmat = jnp.where(causal, jnp.exp(seg), 0.0)
                    ms.append((cb_ * lmat * dt_t[h:h + 1, :]).astype(BF16))
                c0 = g * gw + hp * 2 * P
                xp = xs_b[:, c0:c0 + 2 * P]
                zero = jnp.zeros_like(xp)
                rhs = jnp.concatenate([jnp.where(lane_lo, xp, zero), jnp.where(lane_lo, zero, xp)], axis=0)
                parts.append(_dot(jnp.concatenate(ms, axis=1), rhs))
            y_diag = jnp.concatenate(parts, axis=1)
            st_prev = state_scr[:, gcols]
            y_off = _dot(cg_, st_prev.astype(BF16)) * e_decay_x[:, gcols]
            st_new = _dot_tn(bg_, xd_b[:, gcols])
            state_scr[:, gcols] = st_prev * e_decay_x[L - 1:L, gcols] + st_new
            yg = y_diag + y_off + dexp[:, gcols] * xs[:, gcols]
            yg = yg * _silu(z_scr[rows_c, gcols])
            yg = yg * lax.rsqrt(jnp.mean(yg * yg, axis=-1, keepdims=True) + EPS)
            y_scr[rows_c, gcols] = (yg * nrm[:, gcols]).astype(BF16)

    def ssd_block(bi, carry):
        r0 = pl.multiple_of(bi * blk, blk)
        x = x_ref[pl.ds(r0, blk), :]
        hn = _norm_mod(x, ng, mod[0:1, :], mod[1:2, :]).astype(BF16)
        z_scr[...] = _dot(hn, win_ref[:, 0:din])
        xpad_scr[halo:halo + blk, :] = _dot(hn, win_ref[:, din:din + cdim])
        dt_scr[...] = _dot(hn, win_ref[:, din + cdim:])
        win = xpad_scr[...]
        acc = cb + cw[kc - 1:kc, :] * win[halo:halo + blk, :]
        for k in range(kc - 1):
            acc = acc + cw[k:k + 1, :] * pltpu.roll(win, kc - 1 - k, axis=0)[halo:halo + blk, :]
        act_scr[...] = _silu(acc)
        xpad_scr[0:halo, :] = xpad_scr[blk:blk + halo, :]
        for c in range(blk // L):
            ssd_core(slice(c * L, (c + 1) * L))
        out = _dot(y_scr[...], wout_ref[...])
        o_ref[pl.ds(r0, blk), :] = x + mod[2:3, :] * out
        return carry

    lax.fori_loop(0, rows // blk, ssd_block, 0)


def _ssd_layer(x, mod, ng, win, cw, cb, dtb, alog, dexp, nrm, expand, wout, *, layer, mixer, rows=1024, blk=256):
    bsz, s, d = x.shape
    din = wout.shape[1]
    cdim = cw.shape[2]
    res = [win, cw, cb, dtb, alog, dexp, nrm]
    return pl.pallas_call(
        _ssd_body,
        out_shape=jax.ShapeDtypeStruct(x.shape, F32),
        grid=(bsz, s // rows),
        in_specs=[
            pl.BlockSpec((None, rows, d), lambda b, j: (b, j, 0)),
            pl.BlockSpec((None, None, 6, d), lambda b, j: (layer, b, 0, 0)),
            _resident(ng.shape, layer),
        ] + [_resident(a.shape, mixer) for a in res] + [_resident(expand.shape), _resident(wout.shape, mixer)],
        out_specs=pl.BlockSpec((None, rows, d), lambda b, j: (b, j, 0)),
        scratch_shapes=[
            pltpu.VMEM((blk, din), F32),
            pltpu.VMEM((V7X_SUBLANES + blk, cdim), F32),
            pltpu.VMEM((blk, cdim), F32),
            pltpu.VMEM((blk, dtb.shape[2]), F32),
            pltpu.VMEM((blk, din), BF16),
            pltpu.VMEM((SSD_STATE, din), F32),
        ],
        compiler_params=_params(("parallel", "arbitrary")),
        name="ssd_layer",
    )(x, mod, ng, *res, expand, wout)


def _route(logits_t, n_groups):
    ne, tm = logits_t.shape
    per = ne // n_groups
    ex = jnp.exp(logits_t - jnp.max(logits_t, axis=0, keepdims=True))
    probs = ex / jnp.sum(ex, axis=0, keepdims=True)
    p3 = probs.reshape(n_groups, per, tm)
    io = lax.broadcasted_iota(jnp.int32, p3.shape, 1)
    m1 = jnp.max(p3, axis=1, keepdims=True)
    sel1 = io == jnp.min(jnp.where(p3 == m1, io, per), axis=1, keepdims=True)
    rest = jnp.where(sel1, -1.0, p3)
    m2 = jnp.max(rest, axis=1, keepdims=True)
    sel2 = io == jnp.min(jnp.where(rest == m2, io, per), axis=1, keepdims=True)
    score = m1 + m2
    gio = lax.broadcasted_iota(jnp.int32, score.shape, 0)
    best = jnp.min(jnp.where(score == jnp.max(score, axis=0, keepdims=True), gio, n_groups),
                   axis=0, keepdims=True)
    top = jnp.where(sel1, m1, 0.0) + jnp.where(sel2, m2, 0.0)
    in_best = gio == best
    gates = jnp.sum(jnp.where(in_best, top / score, 0.0), axis=0)
    return gates, in_best.astype(F32).reshape(n_groups, tm)


def _moe_body(x_ref, mod_ref, ng_ref, rw2_ref, rwh_ref, rb_ref, wg_ref, wu_ref, wd_ref, fg_ref,
              o_ref, reg_scr, greg_scr, perm_scr, srt_scr, gsrt_scr, hi_scr, tab_smem, cnt_smem, *, final_norm):
    grp = pl.program_id(2)
    tm, d = x_ref.shape
    per = wg_ref.shape[0]
    ngroups = N_EXPERT_GROUPS
    nsub = tm // MOE_SUB
    sp = perm_scr.shape[1]
    mod = mod_ref[...]

    first = (pl.program_id(0) == 0) & (pl.program_id(1) == 0) & (grp == 0)

    @pl.when(first)
    def _():
        reg_scr[...] = jnp.zeros_like(reg_scr)
        greg_scr[...] = jnp.zeros_like(greg_scr)
        srt_scr[...] = jnp.zeros_like(srt_scr)
        gsrt_scr[...] = jnp.zeros_like(gsrt_scr)

    @pl.when(grp == 0)
    def _():
        for k in range(ngroups):
            cnt_smem[k] = 0
        tok_i = lax.broadcasted_iota(jnp.int32, (MOE_SUB, MOE_SUB), 0)
        tok_j = lax.broadcasted_iota(jnp.int32, (MOE_SUB, MOE_SUB), 1)
        triu = (tok_i <= tok_j).astype(BF16)
        row_i = lax.broadcasted_iota(jnp.int32, (sp, MOE_SUB), 0).astype(F32)
        hn = _norm_mod(x_ref[...], ng_ref[...], mod[3:4, :], mod[4:5, :])
        hi = hn.astype(BF16)
        lo = (hn - hi.astype(F32)).astype(BF16)
        hi_scr[...] = hi
        l2 = _dot(hi, rw2_ref[...])
        logits = l2[:, :V7X_LANES] + l2[:, V7X_LANES:] + _dot(lo, rwh_ref[...]) + rb_ref[...]
        gates4, member = _route(logits.T[0:ngroups * per, :], ngroups)
        g_hi = gates4.astype(BF16).astype(F32)
        g_cat = jnp.concatenate(
            [g_hi, gates4 - g_hi, jnp.zeros((V7X_LANES - 2 * per, tm), F32)], axis=0).astype(BF16)
        member_b = member.astype(BF16)
        subs = [slice(s * MOE_SUB, (s + 1) * MOE_SUB) for s in range(nsub)]
        cums = [_dot(member_b[:, sl], triu) for sl in subs]
        segments = []
        for s in range(nsub):
            cum = cums[s]
            cnt = cum[:, MOE_SUB - 1:MOE_SUB]
            c16 = jnp.floor((cnt + (ROW_ALIGN - 1.0)) * (1.0 / ROW_ALIGN)) * ROW_ALIGN
            bases = [jnp.zeros((1, 1), F32)]
            for k in range(1, ngroups):
                bases.append(bases[-1] + c16[k - 1:k, :])
            base = jnp.concatenate(bases, axis=0)
            pos = jnp.sum(member[:, subs[s]] * (cum - 1.0 + base), axis=0, keepdims=True)
            perm_scr[s] = jnp.where(row_i == pos, 1.0, 0.0).astype(BF16)
            segments.append((base, c16))
        for s in range(nsub):
            srt_scr[s, 0:sp, :] = _dot(perm_scr[s], hi_scr[subs[s], :]).astype(BF16)
            gsrt_scr[s, 0:sp, :] = _dot_nt(perm_scr[s], g_cat[:, subs[s]])
        for s in range(nsub):
            base, c16 = segments[s]
            for k in range(ngroups):
                b0 = pl.multiple_of(base[k, 0].astype(jnp.int32), ROW_ALIGN)
                r0 = pl.multiple_of(cnt_smem[k], ROW_ALIGN)
                t0 = (s * ngroups + k) * 2
                tab_smem[t0] = b0
                tab_smem[t0 + 1] = r0
                reg_scr[k, pl.ds(r0, MOE_SUB), :] = srt_scr[s, pl.ds(b0, MOE_SUB), :]
                greg_scr[k, pl.ds(r0, MOE_SUB), :] = gsrt_scr[s, pl.ds(b0, MOE_SUB), :]
                cnt_smem[k] = r0 + c16[k, 0].astype(jnp.int32)

    def experts(off, m):
        h = reg_scr[grp, pl.ds(off, m), :]
        gt = greg_scr[grp, pl.ds(off, m), :]
        acts = []
        for e in range(per):
            gate = gt[:, e:e + 1] + gt[:, per + e:per + e + 1]
            a = _silu(_dot(h, wg_ref[e])) * _dot(h, wu_ref[e]) * gate
            acts.append(a.astype(BF16))
        out = _dot(jnp.concatenate(acts, axis=1), wd_ref[...])
        reg_scr[grp, pl.ds(off, m), :] = out.astype(BF16)

    n_rows = cnt_smem[grp]
    even = [tm // ngroups + (k + 1) * ROW_ALIGN for k in range(nsub)]
    below = 0
    for m in even:
        @pl.when((n_rows > below) & (n_rows <= m))
        def _(m=m):
            experts(0, m)
        below = m

    @pl.when(n_rows > even[-1])
    def _():
        n_full = n_rows // MOE_CHUNK

        def full_chunk(ci, carry):
            experts(pl.multiple_of(ci * MOE_CHUNK, MOE_CHUNK), MOE_CHUNK)
            return carry

        lax.fori_loop(0, n_full, full_chunk, 0)
        rem = n_rows - n_full * MOE_CHUNK
        tail = pl.multiple_of(n_full * MOE_CHUNK, MOE_CHUNK)

        @pl.when(rem > MOE_CHUNK // 2)
        def _():
            experts(tail, MOE_CHUNK)

        @pl.when((rem > 0) & (rem <= MOE_CHUNK // 2))
        def _():
            experts(tail, MOE_CHUNK // 2)

    @pl.when(grp == ngroups - 1)
    def _():
        for s in range(nsub):
            rows = slice(s * MOE_SUB, (s + 1) * MOE_SUB)
            for k in range(ngroups):
                t0 = (s * ngroups + k) * 2
                b0 = pl.multiple_of(tab_smem[t0], ROW_ALIGN)
                r0 = pl.multiple_of(tab_smem[t0 + 1], ROW_ALIGN)
                srt_scr[s, pl.ds(b0, MOE_SUB), :] = reg_scr[k, pl.ds(r0, MOE_SUB), :]
            out = _dot_tn(perm_scr[s], srt_scr[s, 0:sp, :])
            y = x_ref[rows, :] + mod[5:6, :] * out
            if final_norm:
                y = y * lax.rsqrt(jnp.mean(y * y, axis=-1, keepdims=True) + EPS) * fg_ref[...]
            o_ref[rows, :] = y


def _moe_layer(x, mod, ng, rw2, rwh, rb, wg, wu, wd, fg, *, layer, final_norm, tm=1024):
    bsz, s, d = x.shape
    _, ne, _, de = wg.shape
    ngroups = N_EXPERT_GROUPS
    per = ne // ngroups
    nsub = tm // MOE_SUB
    sp = MOE_SUB + ngroups * ROW_ALIGN
    cap = tm + nsub * ngroups * ROW_ALIGN + max(MOE_CHUNK, MOE_SUB)
    small = [rw2, rwh, rb]
    return pl.pallas_call(
        functools.partial(_moe_body, final_norm=final_norm),
        out_shape=jax.ShapeDtypeStruct(x.shape, F32),
        grid=(bsz, s // tm, ngroups),
        in_specs=[
            pl.BlockSpec((None, tm, d), lambda b, j, g: (b, j, 0)),
            pl.BlockSpec((None, None, 6, d), lambda b, j, g: (layer, b, 0, 0)),
            _resident(ng.shape, layer),
        ] + [_resident(a.shape) for a in small] + [
            pl.BlockSpec((None, per, d, de), lambda b, j, g: (layer, g, 0, 0)),
            pl.BlockSpec((None, per, d, de), lambda b, j, g: (layer, g, 0, 0)),
            pl.BlockSpec((None, None, per * de, d), lambda b, j, g: (layer, g, 0, 0)),
            _resident(fg.shape),
        ],
        out_specs=pl.BlockSpec((None, tm, d), lambda b, j, g: (b, j, 0)),
        scratch_shapes=[
            pltpu.VMEM((ngroups, cap, d), BF16),
            pltpu.VMEM((ngroups, cap, V7X_LANES), F32),
            pltpu.VMEM((nsub, sp, MOE_SUB), BF16),
            pltpu.VMEM((nsub, sp + MOE_SUB, d), BF16),
            pltpu.VMEM((nsub, sp + MOE_SUB, V7X_LANES), F32),
            pltpu.VMEM((tm, d), BF16),
            pltpu.SMEM((nsub * ngroups * 2,), jnp.int32),
            pltpu.SMEM((ngroups,), jnp.int32),
        ],
        compiler_params=_params(("arbitrary", "arbitrary", "arbitrary")),
        name="moe_layer",
    )(x, mod, ng, *small, wg, wu, wd.reshape(wd.shape[0], ngroups, per * de, d), fg)


def kernel(x, c, ada_w, ada_b, norm1_g, norm2_g, lru_w_in, lru_conv_w, lru_conv_b, lru_wa, lru_ba, lru_wx, lru_bx, lru_lambda, lru_w_out, ssd_w_in, ssd_conv_w, ssd_conv_b, ssd_dt_bias, ssd_a_log, ssd_d, ssd_norm_g, ssd_w_out, router_w, router_b, moe_w_gate, moe_w_up, moe_w_down, final_norm_g):
    bsz, s, d = x.shape
    depth = ada_w.shape[0]
    n_mixers = 2
    mod = _ada_mod(c, ada_w, ada_b).reshape(depth, bsz, 6, d)
    ng1 = norm1_g.reshape(depth, 1, d)
    ng2 = norm2_g.reshape(depth, 1, d)

    n_lru, nh = lru_wa.shape[:2]
    w = lru_w_out.shape[1]
    lru_args = (
        lru_w_in.astype(BF16), lru_conv_w, lru_conv_b.reshape(n_lru, 1, w),
        (0.5 * jnp.concatenate([lru_wa, lru_wx], axis=-1)).astype(BF16),
        0.5 * jnp.concatenate([lru_ba.reshape(n_lru, nh, 1, w // nh), lru_bx.reshape(n_lru, nh, 1, w // nh)], axis=-1),
        lru_lambda.reshape(n_lru, 1, w), lru_w_out.astype(BF16))

    n_ssd, heads = ssd_dt_bias.shape
    din = heads * SSD_HEAD_DIM
    cdim = din + 2 * SSD_GROUPS * SSD_STATE
    expand = (jnp.arange(heads)[:, None] == (jnp.arange(din) // SSD_HEAD_DIM)[None, :]).astype(BF16)
    ssd_args = (
        ssd_w_in.astype(BF16), ssd_conv_w, ssd_conv_b.reshape(n_ssd, 1, cdim),
        ssd_dt_bias.reshape(n_ssd, 1, heads), ssd_a_log.reshape(n_ssd, 1, heads),
        jnp.repeat(ssd_d, SSD_HEAD_DIM, axis=1).reshape(n_ssd, 1, din),
        ssd_norm_g.reshape(n_ssd, 1, din), expand, ssd_w_out.astype(BF16))
    moe_args = (moe_w_gate.astype(BF16), moe_w_up.astype(BF16), moe_w_down.astype(BF16))

    ne = router_w.shape[1]
    rw = jnp.pad(router_w, ((0, 0), (0, V7X_LANES - ne)))
    rwh = rw.astype(BF16)
    rw2 = jnp.concatenate([rwh, (rw - rwh.astype(F32)).astype(BF16)], axis=1)
    rb = jnp.pad(router_b, (0, V7X_LANES - ne)).reshape(1, V7X_LANES)
    fg = final_norm_g.reshape(1, d)

    for i in range(depth):
        j = i // n_mixers
        if i % n_mixers == 0:
            x = _lru_layer(x, mod, ng1, *lru_args, layer=i, mixer=j)
        else:
            x = _ssd_layer(x, mod, ng1, *ssd_args, layer=i, mixer=j)
        x = _moe_layer(x, mod, ng2, rw2, rwh, rb, *moe_args, fg, layer=i, final_norm=(i == depth - 1))
    return x
```

```python
import functools

import jax
import jax.numpy as jnp
from jax import lax
from jax.experimental import pallas as pl
from jax.experimental.pallas import tpu as pltpu

F32 = jnp.float32
BF16 = jnp.bfloat16

EPS = 1e-6
LRU_C = 8.0
LOG2_E = 1.4426950408889634
SSD_CHUNK = 128
SSD_HEAD_DIM = 64
SSD_GROUPS = 4
SSD_STATE = 128
N_EXPERT_GROUPS = 4

V7X_LANES = 128
V7X_SUBLANES = 8
V7X_VMEM_LIMIT_BYTES = 56 * 1024 * 1024
V7X_BF16_ROWS = 16

MOE_SUB = 256
ROW_ALIGN = V7X_BF16_ROWS
MOE_CHUNK = 256

Z_COLS = 256


def _params(semantics):
    return pltpu.CompilerParams(dimension_semantics=semantics, vmem_limit_bytes=V7X_VMEM_LIMIT_BYTES)


def _resident(shape, layer=None):
    if layer is None:
        return pl.BlockSpec(shape, lambda *_: (0,) * len(shape), pipeline_mode=pl.Buffered(1))
    return pl.BlockSpec((None,) + tuple(shape[1:]), lambda *_: (layer,) + (0,) * (len(shape) - 1),
                        pipeline_mode=pl.Buffered(1))


def _norm_mod(x, g, shift, scale):
    y = x * lax.rsqrt(jnp.mean(x * x, axis=-1, keepdims=True) + EPS)
    return (y * g) * (1.0 + scale) + shift


def _softplus(z):
    return jnp.maximum(z, 0.0) + jnp.log1p(jnp.exp(-jnp.abs(z)))


def _sigmoid(z):
    return 0.5 + 0.5 * jnp.tanh(0.5 * z)


def _silu(z):
    return z * _sigmoid(z)


def _dot(a, b):
    return jnp.dot(a, b, preferred_element_type=F32)


def _dot_nt(a, b, precision=None):
    return lax.dot_general(a, b, (((1,), (1,)), ((), ())), precision=precision, preferred_element_type=F32)


def _dot_tn(a, b):
    return lax.dot_general(a, b, (((0,), (0,)), ((), ())), preferred_element_type=F32)


def _ada_body(c_ref, w_ref, b_ref, o_ref):
    cond = _silu(c_ref[...]).astype(BF16)
    o_ref[...] = _dot(cond, w_ref[...].astype(BF16)) + b_ref[...]


def _ada_mod(c, ada_w, ada_b, tn=3072):
    depth, d, n = ada_w.shape
    bsz = c.shape[0]
    return pl.pallas_call(
        _ada_body,
        out_shape=jax.ShapeDtypeStruct((depth, bsz, n), F32),
        grid=(depth, n // tn),
        in_specs=[
            pl.BlockSpec((bsz, d), lambda l, j: (0, 0)),
            pl.BlockSpec((None, d, tn), lambda l, j: (l, 0, j)),
            pl.BlockSpec((None, 1, tn), lambda l, j: (l, 0, j)),
        ],
        out_specs=pl.BlockSpec((None, bsz, tn), lambda l, j: (l, 0, j)),
        compiler_params=_params(("parallel", "parallel")),
        name="ada_mod",
    )(c, ada_w, ada_b.reshape(depth, 1, n))


def _lru_body(x_ref, mod_ref, ng_ref, win_ref, cw_ref, cb_ref, wg_ref, bg_ref, lam_ref, wout_ref,
              o_ref, xpad_scr, h_scr, *, tc):
    nb, ts, d = x_ref.shape
    w = xpad_scr.shape[1]
    rc = nb * tc
    nh = wg_ref.shape[0]
    hw = w // nh
    kc = cw_ref.shape[0]
    halo = (kc - 1) * nb

    @pl.when(pl.program_id(0) == 0)
    def _():
        xpad_scr[0:halo, :] = jnp.zeros((halo, w), F32)
        h_scr[...] = jnp.zeros_like(h_scr)

    mod = mod_ref[...]
    ng = ng_ref[...]
    cw = cw_ref[...]
    cb = cb_ref[...]
    sp_c = (-LRU_C) * _softplus(-lam_ref[...])
    sp_c2 = sp_c * LOG2_E

    def chunk(ci, h):
        t0 = pl.multiple_of(ci * tc, tc)
        x = x_ref[:, pl.ds(t0, tc), :]
        hn = _norm_mod(x, ng, mod[:, 0:1, :], mod[:, 1:2, :]).reshape(rc, d).astype(BF16)
        xbr = _dot(hn, win_ref[:, w:])
        xpad_scr[halo:halo + rc, :] = jnp.swapaxes(xbr.reshape(nb, tc, w), 0, 1).reshape(rc, w)
        xc = cb
        for k in range(kc):
            xc = xc + cw[k:k + 1, :] * xpad_scr[k * nb:k * nb + rc, :]
        xpad_scr[0:halo, :] = xpad_scr[rc:rc + halo, :]
        a_parts, b_parts, gate_parts = [], [], []
        for hd in range(nh):
            if hd % 2 == 0:
                gate_parts.append(_dot(hn, win_ref[:, hd * hw:(hd + 2) * hw]))
            cols = slice(hd * hw, (hd + 1) * hw)
            xh = xc[:, cols]
            ri = _dot(xh.astype(BF16), wg_ref[hd]) + bg_ref[hd]
            r = 0.5 + 0.5 * jnp.tanh(ri[:, :hw])
            i = 0.5 + 0.5 * jnp.tanh(ri[:, hw:])
            log_a = r * sp_c[:, cols]
            q = -2.0 * jnp.tanh(log_a)
            a_parts.append(jnp.exp2(r * sp_c2[:, cols]))
            coef = jnp.where(q > 0.0, q * lax.rsqrt(q * (1.0 + 0.5 * q)), 0.0)
            b_parts.append(coef * (i * xh))
        a = jnp.concatenate(a_parts, axis=1)
        b = jnp.concatenate(b_parts, axis=1)
        hs = []
        for ti in range(tc):
            h = a[ti * nb:(ti + 1) * nb, :] * h + b[ti * nb:(ti + 1) * nb, :]
            hs.append(h)
        hs = jnp.swapaxes(jnp.concatenate(hs, axis=0).reshape(tc, nb, w), 0, 1).reshape(rc, w)
        gate = jnp.concatenate(gate_parts, axis=1)
        y = (hs * jax.nn.gelu(gate, approximate=True)).astype(BF16)
        out = _dot(y, wout_ref[...]).reshape(nb, tc, d)
        o_ref[:, pl.ds(t0, tc), :] = x + mod[:, 2:3, :] * out
        return h

    h_scr[...] = lax.fori_loop(0, ts // tc, chunk, h_scr[...])


def _lru_layer(x, mod, ng, win, cw, cb, wg, bg, lam, wout, *, layer, mixer, ts=128, tc=64):
    bsz, s, d = x.shape
    w = wout.shape[1]
    kc = cw.shape[1]
    return pl.pallas_call(
        functools.partial(_lru_body, tc=tc),
        out_shape=jax.ShapeDtypeStruct(x.shape, F32),
        grid=(s // ts,),
        in_specs=[
            pl.BlockSpec((bsz, ts, d), lambda j: (0, j, 0)),
            _resident(mod.shape, layer), _resident(ng.shape, layer),
        ] + [_resident(a.shape, mixer) for a in (win, cw, cb, wg, bg, lam, wout)],
        out_specs=pl.BlockSpec((bsz, ts, d), lambda j: (0, j, 0)),
        scratch_shapes=[
            pltpu.VMEM(((kc - 1 + tc) * bsz, w), F32),
            pltpu.VMEM((bsz, w), F32),
        ],
        compiler_params=_params(("arbitrary",)),
        name="lru_layer",
    )(x, mod, ng, win, cw, cb, wg, bg, lam, wout)


def _ssd_body(x_ref, mod_ref, ng_ref, win_ref, cw_ref, cb_ref, dtb_ref, alog_ref,
              dexp_ref, nrm_ref, expand_ref, wout_ref,
              o_ref, z_scr, xpad_scr, act_scr, dt_scr, y_scr, state_scr):
    rows, d = x_ref.shape
    blk = z_scr.shape[0]
    din = z_scr.shape[1]
    cdim = act_scr.shape[1]
    kc = cw_ref.shape[0]
    halo = V7X_SUBLANES
    L, P, G, N = SSD_CHUNK, SSD_HEAD_DIM, SSD_GROUPS, SSD_STATE
    gw = din // G
    hpg = gw // P

    @pl.when(pl.program_id(1) == 0)
    def _():
        xpad_scr[0:halo, :] = jnp.zeros((halo, cdim), F32)
        state_scr[...] = jnp.zeros_like(state_scr)

    mod = mod_ref[...]
    ng = ng_ref[...]
    cw = cw_ref[...]
    cb = cb_ref[...]
    neg_a = -jnp.exp(alog_ref[...])
    dtb = dtb_ref[...]
    dexp = dexp_ref[...]
    nrm = nrm_ref[...]
    expand = expand_ref[...]
    row_i = lax.broadcasted_iota(jnp.int32, (L, L), 0)
    col_i = lax.broadcasted_iota(jnp.int32, (L, L), 1)
    causal = row_i >= col_i
    tril = causal.astype(F32)
    lane_lo = lax.broadcasted_iota(jnp.int32, (L, 2 * P), 1) < P

    def ssd_core(rows_c):
        dt = _softplus(dt_scr[rows_c, :] + dtb)
        da = dt * neg_a
        acs = jnp.dot(tril, da, precision=lax.Precision.HIGHEST, preferred_element_type=F32)
        acs_t = acs.T
        dt_t = dt.T
        last = acs[L - 1:L, :]
        e_decay = jnp.exp(acs)
        w_state = dt * jnp.exp(last - acs)
        ex = _dot(jnp.concatenate([e_decay, w_state], axis=0).astype(BF16), expand)
        e_decay_x = ex[:L]
        w_state_x = ex[L:]
        xs = act_scr[rows_c, 0:din]
        xs_b = xs.astype(BF16)
        xd_b = (xs * w_state_x).astype(BF16)
        for g in range(G):
            bg_ = act_scr[rows_c, din + g * N: din + (g + 1) * N].astype(BF16)
            cg_ = act_scr[rows_c, din + (G + g) * N: din + (G + g + 1) * N].astype(BF16)
            cb_ = _dot_nt(cg_, bg_)
            gcols = slice(g * gw, (g + 1) * gw)
            parts = []
            for hp in range(hpg // 2):
                ms = []
                for hh in range(2):
                    h = g * hpg + 2 * hp + hh
                    seg = acs[:, h:h + 1] - acs_t[h:h + 1, :]
                    lmat = jnp.where(causal, jnp.exp(seg), 0.0)
                    ms.append((cb_ * lmat * dt_t[h:h + 1, :]).astype(BF16))
                c0 = g * gw + hp * 2 * P
                xp = xs_b[:, c0:c0 + 2 * P]
                zero = jnp.zeros_like(xp)
                rhs = jnp.concatenate([jnp.where(lane_lo, xp, zero), jnp.where(lane_lo, zero, xp)], axis=0)
                parts.append(_dot(jnp.concatenate(ms, axis=1), rhs))
            y_diag = jnp.concatenate(parts, axis=1)
            st_prev = state_scr[:, gcols]
            y_off = _dot(cg_, st_prev.astype(BF16)) * e_decay_x[:, gcols]
            st_new = _dot_tn(bg_, xd_b[:, gcols])
            state_scr[:, gcols] = st_prev * e_decay_x[L - 1:L, gcols] + st_new
            yg = y_diag + y_off + dexp[:, gcols] * xs[:, gcols]
            yg = yg * _silu(z_scr[rows_c, gcols])
            yg = yg * lax.rsqrt(jnp.mean(yg * yg, axis=-1, keepdims=True) + EPS)
            y_scr[rows_c, gcols] = (yg * nrm[:, gcols]).astype(BF16)

    def ssd_block(bi, carry):
        r0 = pl.multiple_of(bi * blk, blk)
        x = x_ref[pl.ds(r0, blk), :]
        hn = _norm_mod(x, ng, mod[0:1, :], mod[1:2, :]).astype(BF16)
        xpad_scr[halo:halo + blk, :] = _dot(hn, win_ref[:, din:din + cdim])
        dt_scr[...] = _dot(hn, win_ref[:, din + cdim:])
        conv_tiles = cdim // V7X_LANES
        z_tiles = din // Z_COLS
        for zi in range(z_tiles):
            zs = slice(zi * Z_COLS, (zi + 1) * Z_COLS)
            z_scr[:, zs] = _dot(hn, win_ref[:, zs])
            for ci in range(zi * conv_tiles // z_tiles, (zi + 1) * conv_tiles // z_tiles):
                cs = slice(ci * V7X_LANES, (ci + 1) * V7X_LANES)
                win = xpad_scr[:, cs]
                acc = cb[:, cs] + cw[kc - 1:kc, cs] * win[halo:halo + blk, :]
                for k in range(kc - 1):
                    acc = acc + cw[k:k + 1, cs] * pltpu.roll(win, kc - 1 - k, axis=0)[halo:halo + blk, :]
                act_scr[:, cs] = _silu(acc)
        xpad_scr[0:halo, :] = xpad_scr[blk:blk + halo, :]
        for c in range(blk // L):
            ssd_core(slice(c * L, (c + 1) * L))
        out = _dot(y_scr[...], wout_ref[...])
        o_ref[pl.ds(r0, blk), :] = x + mod[2:3, :] * out
        return carry

    lax.fori_loop(0, rows // blk, ssd_block, 0)


def _ssd_layer(x, mod, ng, win, cw, cb, dtb, alog, dexp, nrm, expand, wout, *, layer, mixer, rows=1024, blk=256):
    bsz, s, d = x.shape
    din = wout.shape[1]
    cdim = cw.shape[2]
    res = [win, cw, cb, dtb, alog, dexp, nrm]
    return pl.pallas_call(
        _ssd_body,
        out_shape=jax.ShapeDtypeStruct(x.shape, F32),
        grid=(bsz, s // rows),
        in_specs=[
            pl.BlockSpec((None, rows, d), lambda b, j: (b, j, 0)),
            pl.BlockSpec((None, None, 6, d), lambda b, j: (layer, b, 0, 0)),
            _resident(ng.shape, layer),
        ] + [_resident(a.shape, mixer) for a in res] + [_resident(expand.shape), _resident(wout.shape, mixer)],
        out_specs=pl.BlockSpec((None, rows, d), lambda b, j: (b, j, 0)),
        scratch_shapes=[
            pltpu.VMEM((blk, din), F32),
            pltpu.VMEM((V7X_SUBLANES + blk, cdim), F32),
            pltpu.VMEM((blk, cdim), F32),
            pltpu.VMEM((blk, dtb.shape[2]), F32),
            pltpu.VMEM((blk, din), BF16),
            pltpu.VMEM((SSD_STATE, din), F32),
        ],
        compiler_params=_params(("parallel", "arbitrary")),
        name="ssd_layer",
    )(x, mod, ng, *res, expand, wout)


def _route(logits_t, n_groups):
    ne, tm = logits_t.shape
    per = ne // n_groups
    ex = jnp.exp(logits_t - jnp.max(logits_t, axis=0, keepdims=True))
    probs = ex / jnp.sum(ex, axis=0, keepdims=True)
    p3 = probs.reshape(n_groups, per, tm)
    io = lax.broadcasted_iota(jnp.int32, p3.shape, 1)
    m1 = jnp.max(p3, axis=1, keepdims=True)
    sel1 = io == jnp.min(jnp.where(p3 == m1, io, per), axis=1, keepdims=True)
    rest = jnp.where(sel1, -1.0, p3)
    m2 = jnp.max(rest, axis=1, keepdims=True)
    sel2 = io == jnp.min(jnp.where(rest == m2, io, per), axis=1, keepdims=True)
    score = m1 + m2
    gio = lax.broadcasted_iota(jnp.int32, score.shape, 0)
    best = jnp.min(jnp.where(score == jnp.max(score, axis=0, keepdims=True), gio, n_groups),
                   axis=0, keepdims=True)
    top = jnp.where(sel1, m1, 0.0) + jnp.where(sel2, m2, 0.0)
    in_best = gio == best
    gates = jnp.sum(jnp.where(in_best, top / score, 0.0), axis=0)
    return gates, in_best.astype(F32).reshape(n_groups, tm)


def _moe_body(x_ref, mod_ref, ng_ref, rw2_ref, rwh_ref, rb_ref, wg_ref, wu_ref, wd_ref, fg_ref,
              o_ref, reg_scr, greg_scr, perm_scr, srt_scr, gsrt_scr, hi_scr, tab_smem, cnt_smem, *, final_norm):
    grp = pl.program_id(2)
    tm, d = x_ref.shape
    per = wg_ref.shape[0]
    ngroups = N_EXPERT_GROUPS
    nsub = tm // MOE_SUB
    sp = perm_scr.shape[1]
    mod = mod_ref[...]

    first = (pl.program_id(0) == 0) & (pl.program_id(1) == 0) & (grp == 0)

    @pl.when(first)
    def _():
        reg_scr[...] = jnp.zeros_like(reg_scr)
        greg_scr[...] = jnp.zeros_like(greg_scr)
        srt_scr[...] = jnp.zeros_like(srt_scr)
        gsrt_scr[...] = jnp.zeros_like(gsrt_scr)

    @pl.when(grp == 0)
    def _():
        for k in range(ngroups):
            cnt_smem[k] = 0
        tok_i = lax.broadcasted_iota(jnp.int32, (MOE_SUB, MOE_SUB), 0)
        tok_j = lax.broadcasted_iota(jnp.int32, (MOE_SUB, MOE_SUB), 1)
        triu = (tok_i <= tok_j).astype(BF16)
        row_i = lax.broadcasted_iota(jnp.int32, (sp, MOE_SUB), 0).astype(F32)
        hn = _norm_mod(x_ref[...], ng_ref[...], mod[3:4, :], mod[4:5, :])
        hi = hn.astype(BF16)
        lo = (hn - hi.astype(F32)).astype(BF16)
        hi_scr[...] = hi
        l2 = _dot(hi, rw2_ref[...])
        logits = l2[:, :V7X_LANES] + l2[:, V7X_LANES:] + _dot(lo, rwh_ref[...]) + rb_ref[...]
        gates4, member = _route(logits.T[0:ngroups * per, :], ngroups)
        g_hi = gates4.astype(BF16).astype(F32)
        g_cat = jnp.concatenate(
            [g_hi, gates4 - g_hi, jnp.zeros((V7X_LANES - 2 * per, tm), F32)], axis=0).astype(BF16)
        member_b = member.astype(BF16)
        subs = [slice(s * MOE_SUB, (s + 1) * MOE_SUB) for s in range(nsub)]
        cums = [_dot(member_b[:, sl], triu) for sl in subs]
        segments = []
        for s in range(nsub):
            cum = cums[s]
            cnt = cum[:, MOE_SUB - 1:MOE_SUB]
            c16 = jnp.floor((cnt + (ROW_ALIGN - 1.0)) * (1.0 / ROW_ALIGN)) * ROW_ALIGN
            bases = [jnp.zeros((1, 1), F32)]
            for k in range(1, ngroups):
                bases.append(bases[-1] + c16[k - 1:k, :])
            base = jnp.concatenate(bases, axis=0)
            pos = jnp.sum(member[:, subs[s]] * (cum - 1.0 + base), axis=0, keepdims=True)
            perm_scr[s] = jnp.where(row_i == pos, 1.0, 0.0).astype(BF16)
            segments.append((base, c16))
        for s in range(nsub):
            srt_scr[s % 2, 0:sp, :] = _dot(perm_scr[s], hi_scr[subs[s], :]).astype(BF16)
            gsrt_scr[s % 2, 0:sp, :] = _dot_nt(perm_scr[s], g_cat[:, subs[s]])
            base, c16 = segments[s]
            for k in range(ngroups):
                b0 = pl.multiple_of(base[k, 0].astype(jnp.int32), ROW_ALIGN)
                r0 = pl.multiple_of(cnt_smem[k], ROW_ALIGN)
                t0 = (s * ngroups + k) * 2
                tab_smem[t0] = b0
                tab_smem[t0 + 1] = r0
                reg_scr[k, pl.ds(r0, MOE_SUB), :] = srt_scr[s % 2, pl.ds(b0, MOE_SUB), :]
                greg_scr[k, pl.ds(r0, MOE_SUB), :] = gsrt_scr[s % 2, pl.ds(b0, MOE_SUB), :]
                cnt_smem[k] = r0 + c16[k, 0].astype(jnp.int32)

    def experts(off, m):
        h = reg_scr[grp, pl.ds(off, m), :]
        gt = greg_scr[grp, pl.ds(off, m), :]
        acts = []
        for e in range(per):
            gate = gt[:, e:e + 1] + gt[:, per + e:per + e + 1]
            a = _silu(_dot(h, wg_ref[e])) * _dot(h, wu_ref[e]) * gate
            acts.append(a.astype(BF16))
        out = _dot(jnp.concatenate(acts, axis=1), wd_ref[...].astype(BF16))
        reg_scr[grp, pl.ds(off, m), :] = out.astype(BF16)

    n_rows = cnt_smem[grp]
    even = [tm // ngroups + (k + 1) * ROW_ALIGN for k in range(nsub)]
    below = 0
    for m in even:
        @pl.when((n_rows > below) & (n_rows <= m))
        def _(m=m):
            experts(0, m)
        below = m

    @pl.when(n_rows > even[-1])
    def _():
        n_full = n_rows // MOE_CHUNK

        def full_chunk(ci, carry):
            experts(pl.multiple_of(ci * MOE_CHUNK, MOE_CHUNK), MOE_CHUNK)
            return carry

        lax.fori_loop(0, n_full, full_chunk, 0)
        rem = n_rows - n_full * MOE_CHUNK
        tail = pl.multiple_of(n_full * MOE_CHUNK, MOE_CHUNK)

        @pl.when(rem > MOE_CHUNK // 2)
        def _():
            experts(tail, MOE_CHUNK)

        @pl.when((rem > 0) & (rem <= MOE_CHUNK // 2))
        def _():
            experts(tail, MOE_CHUNK // 2)

    @pl.when(grp == ngroups - 1)
    def _():
        for s in range(nsub):
            rows = slice(s * MOE_SUB, (s + 1) * MOE_SUB)
            for k in range(ngroups):
                t0 = (s * ngroups + k) * 2
                b0 = pl.multiple_of(tab_smem[t0], ROW_ALIGN)
                r0 = pl.multiple_of(tab_smem[t0 + 1], ROW_ALIGN)
                srt_scr[s % 2, pl.ds(b0, MOE_SUB), :] = reg_scr[k, pl.ds(r0, MOE_SUB), :]
            out = _dot_tn(perm_scr[s], srt_scr[s % 2, 0:sp, :])
            y = x_ref[rows, :] + mod[5:6, :] * out
            if final_norm:
                y = y * lax.rsqrt(jnp.mean(y * y, axis=-1, keepdims=True) + EPS) * fg_ref[...]
            o_ref[rows, :] = y


def _moe_layer(x, mod, ng, rw2, rwh, rb, wg, wu, wd, fg, *, layer, final_norm, tm=1024):
    bsz, s, d = x.shape
    _, ne, _, de = wg.shape
    ngroups = N_EXPERT_GROUPS
    per = ne // ngroups
    nsub = tm // MOE_SUB
    sp = MOE_SUB + ngroups * ROW_ALIGN
    cap = tm + nsub * ngroups * ROW_ALIGN + max(MOE_CHUNK, MOE_SUB)
    small = [rw2, rwh, rb]
    return pl.pallas_call(
        functools.partial(_moe_body, final_norm=final_norm),
        out_shape=jax.ShapeDtypeStruct(x.shape, F32),
        grid=(bsz, s // tm, ngroups),
        in_specs=[
            pl.BlockSpec((None, tm, d), lambda b, j, g: (b, j, 0)),
            pl.BlockSpec((None, None, 6, d), lambda b, j, g: (layer, b, 0, 0)),
            _resident(ng.shape, layer),
        ] + [_resident(a.shape) for a in small] + [
            pl.BlockSpec((None, per, d, de), lambda b, j, g: (layer, g, 0, 0)),
            pl.BlockSpec((None, per, d, de), lambda b, j, g: (layer, g, 0, 0)),
            pl.BlockSpec((None, None, per * de, d), lambda b, j, g: (layer, g, 0, 0)),
            _resident(fg.shape),
        ],
        out_specs=pl.BlockSpec((None, tm, d), lambda b, j, g: (b, j, 0)),
        scratch_shapes=[
            pltpu.VMEM((ngroups, cap, d), BF16),
            pltpu.VMEM((ngroups, cap, V7X_LANES), F32),
            pltpu.VMEM((nsub, sp, MOE_SUB), BF16),
            pltpu.VMEM((2, sp + MOE_SUB, d), BF16),
            pltpu.VMEM((2, sp + MOE_SUB, V7X_LANES), F32),
            pltpu.VMEM((tm, d), BF16),
            pltpu.SMEM((nsub * ngroups * 2,), jnp.int32),
            pltpu.SMEM((ngroups,), jnp.int32),
        ],
        compiler_params=_params(("arbitrary", "arbitrary", "arbitrary")),
        name="moe_layer",
    )(x, mod, ng, *small, wg, wu, wd.reshape(wd.shape[0], ngroups, per * de, d), fg)


def kernel(x, c, ada_w, ada_b, norm1_g, norm2_g, lru_w_in, lru_conv_w, lru_conv_b, lru_wa, lru_ba, lru_wx, lru_bx, lru_lambda, lru_w_out, ssd_w_in, ssd_conv_w, ssd_conv_b, ssd_dt_bias, ssd_a_log, ssd_d, ssd_norm_g, ssd_w_out, router_w, router_b, moe_w_gate, moe_w_up, moe_w_down, final_norm_g):
    bsz, s, d = x.shape
    depth = ada_w.shape[0]
    n_mixers = 2
    mod = _ada_mod(c, ada_w, ada_b).reshape(depth, bsz, 6, d)
    ng1 = norm1_g.reshape(depth, 1, d)
    ng2 = norm2_g.reshape(depth, 1, d)

    n_lru, nh = lru_wa.shape[:2]
    w = lru_w_out.shape[1]
    lru_args = (
        lru_w_in.astype(BF16), lru_conv_w, lru_conv_b.reshape(n_lru, 1, w),
        (0.5 * jnp.concatenate([lru_wa, lru_wx], axis=-1)).astype(BF16),
        0.5 * jnp.concatenate([lru_ba.reshape(n_lru, nh, 1, w // nh), lru_bx.reshape(n_lru, nh, 1, w // nh)], axis=-1),
        lru_lambda.reshape(n_lru, 1, w), lru_w_out.astype(BF16))

    n_ssd, heads = ssd_dt_bias.shape
    din = heads * SSD_HEAD_DIM
    cdim = din + 2 * SSD_GROUPS * SSD_STATE
    expand = (jnp.arange(heads)[:, None] == (jnp.arange(din) // SSD_HEAD_DIM)[None, :]).astype(BF16)
    ssd_args = (
        ssd_w_in.astype(BF16), ssd_conv_w, ssd_conv_b.reshape(n_ssd, 1, cdim),
        ssd_dt_bias.reshape(n_ssd, 1, heads), ssd_a_log.reshape(n_ssd, 1, heads),
        jnp.repeat(ssd_d, SSD_HEAD_DIM, axis=1).reshape(n_ssd, 1, din),
        ssd_norm_g.reshape(n_ssd, 1, din), expand, ssd_w_out.astype(BF16))
    moe_args = (moe_w_gate.astype(BF16), moe_w_up.astype(BF16), moe_w_down)

    ne = router_w.shape[1]
    rw = jnp.pad(router_w, ((0, 0), (0, V7X_LANES - ne)))
    rwh = rw.astype(BF16)
    rw2 = jnp.concatenate([rwh, (rw - rwh.astype(F32)).astype(BF16)], axis=1)
    rb = jnp.pad(router_b, (0, V7X_LANES - ne)).reshape(1, V7X_LANES)
    fg = final_norm_g.reshape(1, d)

    for i in range(depth):
        j = i // n_mixers
        if i % n_mixers == 0:
            x = _lru_layer(x, mod, ng1, *lru_args, layer=i, mixer=j)
        else:
            x = _ssd_layer(x, mod, ng1, *ssd_args, layer=i, mixer=j)
        x = _moe_layer(x, mod, ng2, rw2, rwh, rb, *moe_args, fg, layer=i, final_norm=(i == depth - 1))
    return x
```

```python
import functools

import jax
import jax.numpy as jnp
from jax import lax
from jax.experimental import pallas as pl
from jax.experimental.pallas import tpu as pltpu

F32 = jnp.float32
BF16 = jnp.bfloat16

EPS = 1e-6
LRU_C = 8.0
LOG2_E = 1.4426950408889634
SSD_CHUNK = 128
SSD_HEAD_DIM = 64
SSD_GROUPS = 4
SSD_STATE = 128
N_EXPERT_GROUPS = 4

V7X_LANES = 128
V7X_SUBLANES = 8
V7X_VMEM_LIMIT_BYTES = 56 * 1024 * 1024
V7X_BF16_ROWS = 16

MOE_SUB = 256
ROW_ALIGN = V7X_BF16_ROWS
MOE_CHUNK = 256

Z_COLS = 256


def _params(semantics):
    return pltpu.CompilerParams(dimension_semantics=semantics, vmem_limit_bytes=V7X_VMEM_LIMIT_BYTES)


def _resident(shape, layer=None):
    if layer is None:
        return pl.BlockSpec(shape, lambda *_: (0,) * len(shape), pipeline_mode=pl.Buffered(1))
    return pl.BlockSpec((None,) + tuple(shape[1:]), lambda *_: (layer,) + (0,) * (len(shape) - 1),
                        pipeline_mode=pl.Buffered(1))


def _norm_mod(x, g, shift, scale):
    y = x * lax.rsqrt(jnp.mean(x * x, axis=-1, keepdims=True) + EPS)
    return (y * g) * (1.0 + scale) + shift


def _softplus(z):
    return jnp.maximum(z, 0.0) + jnp.log1p(jnp.exp(-jnp.abs(z)))


def _sigmoid(z):
    return 0.5 + 0.5 * jnp.tanh(0.5 * z)


def _silu(z):
    return z * _sigmoid(z)


def _dot(a, b):
    return jnp.dot(a, b, preferred_element_type=F32)


def _dot_nt(a, b, precision=None):
    return lax.dot_general(a, b, (((1,), (1,)), ((), ())), precision=precision, preferred_element_type=F32)


def _dot_tn(a, b):
    return lax.dot_general(a, b, (((0,), (0,)), ((), ())), preferred_element_type=F32)


def _ada_body(c_ref, w_ref, b_ref, o_ref):
    cond = _silu(c_ref[...]).astype(BF16)
    o_ref[...] = _dot(cond, w_ref[...].astype(BF16)) + b_ref[...]


def _ada_mod(c, ada_w, ada_b, tn=3072):
    depth, d, n = ada_w.shape
    bsz = c.shape[0]
    return pl.pallas_call(
        _ada_body,
        out_shape=jax.ShapeDtypeStruct((depth, bsz, n), F32),
        grid=(depth, n // tn),
        in_specs=[
            pl.BlockSpec((bsz, d), lambda l, j: (0, 0)),
            pl.BlockSpec((None, d, tn), lambda l, j: (l, 0, j)),
            pl.BlockSpec((None, 1, tn), lambda l, j: (l, 0, j)),
        ],
        out_specs=pl.BlockSpec((None, bsz, tn), lambda l, j: (l, 0, j)),
        compiler_params=_params(("parallel", "parallel")),
        name="ada_mod",
    )(c, ada_w, ada_b.reshape(depth, 1, n))


def _lru_body(x_ref, mod_ref, ng_ref, win_ref, cw_ref, cb_ref, wg_ref, bg_ref, lam_ref, wout_ref,
              o_ref, xpad_scr, h_scr, *, tc):
    nb, ts, d = x_ref.shape
    w = xpad_scr.shape[1]
    rc = nb * tc
    nh = wg_ref.shape[0]
    hw = w // nh
    kc = cw_ref.shape[0]
    halo = (kc - 1) * nb

    @pl.when(pl.program_id(0) == 0)
    def _():
        xpad_scr[0:halo, :] = jnp.zeros((halo, w), F32)
        h_scr[...] = jnp.zeros_like(h_scr)

    mod = mod_ref[...]
    ng = ng_ref[...]
    cw = cw_ref[...]
    cb = cb_ref[...]
    sp_c = (-LRU_C) * _softplus(-lam_ref[...])
    sp_c2 = sp_c * LOG2_E

    def chunk(ci, h):
        t0 = pl.multiple_of(ci * tc, tc)
        x = x_ref[:, pl.ds(t0, tc), :]
        hn = _norm_mod(x, ng, mod[:, 0:1, :], mod[:, 1:2, :]).reshape(rc, d).astype(BF16)
        xbr = _dot(hn, win_ref[:, w:])
        xpad_scr[halo:halo + rc, :] = jnp.swapaxes(xbr.reshape(nb, tc, w), 0, 1).reshape(rc, w)
        xc = cb
        for k in range(kc):
            xc = xc + cw[k:k + 1, :] * xpad_scr[k * nb:k * nb + rc, :]
        xpad_scr[0:halo, :] = xpad_scr[rc:rc + halo, :]
        a_parts, b_parts, gate_parts = [], [], []
        for hd in range(nh):
            if hd % 2 == 0:
                gate_parts.append(_dot(hn, win_ref[:, hd * hw:(hd + 2) * hw]))
            cols = slice(hd * hw, (hd + 1) * hw)
            xh = xc[:, cols]
            ri = _dot(xh.astype(BF16), wg_ref[hd]) + bg_ref[hd]
            r = 0.5 + 0.5 * jnp.tanh(ri[:, :hw])
            i = 0.5 + 0.5 * jnp.tanh(ri[:, hw:])
            log_a = r * sp_c[:, cols]
            q = -2.0 * jnp.tanh(log_a)
            a_parts.append(jnp.exp2(r * sp_c2[:, cols]))
            coef = jnp.where(q > 0.0, q * lax.rsqrt(q * (1.0 + 0.5 * q)), 0.0)
            b_parts.append(coef * (i * xh))
        a = jnp.concatenate(a_parts, axis=1)
        b = jnp.concatenate(b_parts, axis=1)
        hs = []
        for ti in range(tc):
            h = a[ti * nb:(ti + 1) * nb, :] * h + b[ti * nb:(ti + 1) * nb, :]
            hs.append(h)
        hs = jnp.swapaxes(jnp.concatenate(hs, axis=0).reshape(tc, nb, w), 0, 1).reshape(rc, w)
        gate = jnp.concatenate(gate_parts, axis=1)
        y = (hs * jax.nn.gelu(gate, approximate=True)).astype(BF16)
        out = _dot(y, wout_ref[...]).reshape(nb, tc, d)
        o_ref[:, pl.ds(t0, tc), :] = x + mod[:, 2:3, :] * out
        return h

    h_scr[...] = lax.fori_loop(0, ts // tc, chunk, h_scr[...])


def _lru_layer(x, mod, ng, win, cw, cb, wg, bg, lam, wout, *, layer, mixer, ts=128, tc=64):
    bsz, s, d = x.shape
    w = wout.shape[1]
    kc = cw.shape[1]
    return pl.pallas_call(
        functools.partial(_lru_body, tc=tc),
        out_shape=jax.ShapeDtypeStruct(x.shape, F32),
        grid=(s // ts,),
        in_specs=[
            pl.BlockSpec((bsz, ts, d), lambda j: (0, j, 0)),
            _resident(mod.shape, layer), _resident(ng.shape, layer),
        ] + [_resident(a.shape, mixer) for a in (win, cw, cb, wg, bg, lam, wout)],
        out_specs=pl.BlockSpec((bsz, ts, d), lambda j: (0, j, 0)),
        scratch_shapes=[
            pltpu.VMEM(((kc - 1 + tc) * bsz, w), F32),
            pltpu.VMEM((bsz, w), F32),
        ],
        compiler_params=_params(("arbitrary",)),
        name="lru_layer",
    )(x, mod, ng, win, cw, cb, wg, bg, lam, wout)


def _ssd_body(x_ref, mod_ref, ng_ref, win_ref, cw_ref, cb_ref, dtb_ref, alog_ref,
              dexp_ref, nrm_ref, expand_ref, wout_ref,
              o_ref, z_scr, xpad_scr, act_scr, dt_scr, y_scr, state_scr):
    rows, d = x_ref.shape
    blk = z_scr.shape[0]
    din = z_scr.shape[1]
    cdim = act_scr.shape[1]
    kc = cw_ref.shape[0]
    halo = V7X_SUBLANES
    L, P, G, N = SSD_CHUNK, SSD_HEAD_DIM, SSD_GROUPS, SSD_STATE
    gw = din // G
    hpg = gw // P

    @pl.when(pl.program_id(1) == 0)
    def _():
        xpad_scr[0:halo, :] = jnp.zeros((halo, cdim), F32)
        state_scr[...] = jnp.zeros_like(state_scr)

    mod = mod_ref[...]
    ng = ng_ref[...]
    cw = cw_ref[...]
    cb = cb_ref[...]
    neg_a = -jnp.exp(alog_ref[...])
    dtb = dtb_ref[...]
    dexp = dexp_ref[...]
    nrm = nrm_ref[...]
    expand = expand_ref[...]
    row_i = lax.broadcasted_iota(jnp.int32, (L, L), 0)
    col_i = lax.broadcasted_iota(jnp.int32, (L, L), 1)
    causal = row_i >= col_i
    tril = causal.astype(F32)
    lane_lo = lax.broadcasted_iota(jnp.int32, (L, 2 * P), 1) < P

    def ssd_core(rows_c):
        dt = _softplus(dt_scr[rows_c, :] + dtb)
        da = dt * neg_a
        acs = jnp.dot(tril, da, precision=lax.Precision.HIGHEST, preferred_element_type=F32)
        acs_t = acs.T
        dt_t = dt.T
        last = acs[L - 1:L, :]
        e_decay = jnp.exp(acs)
        w_state = dt * jnp.exp(last - acs)
        ex = _dot(jnp.concatenate([e_decay, w_state], axis=0).astype(BF16), expand)
        e_decay_x = ex[:L]
        w_state_x = ex[L:]
        xs = act_scr[rows_c, 0:din]
        xs_b = xs.astype(BF16)
        xd_b = (xs * w_state_x).astype(BF16)
        for g in range(G):
            bg_ = act_scr[rows_c, din + g * N: din + (g + 1) * N].astype(BF16)
            cg_ = act_scr[rows_c, din + (G + g) * N: din + (G + g + 1) * N].astype(BF16)
            cb_ = _dot_nt(cg_, bg_)
            gcols = slice(g * gw, (g + 1) * gw)
            parts = []
            for hp in range(hpg // 2):
                ms = []
                for hh in range(2):
                    h = g * hpg + 2 * hp + hh
                    seg = acs[:, h:h + 1] - acs_t[h:h + 1, :]
                    lmat = jnp.where(causal, jnp.exp(seg), 0.0)
                    ms.append((cb_ * lmat * dt_t[h:h + 1, :]).astype(BF16))
                c0 = g * gw + hp * 2 * P
                xp = xs_b[:, c0:c0 + 2 * P]
                zero = jnp.zeros_like(xp)
                rhs = jnp.concatenate([jnp.where(lane_lo, xp, zero), jnp.where(lane_lo, zero, xp)], axis=0)
                parts.append(_dot(jnp.concatenate(ms, axis=1), rhs))
            y_diag = jnp.concatenate(parts, axis=1)
            st_prev = state_scr[:, gcols]
            y_off = _dot(cg_, st_prev.astype(BF16)) * e_decay_x[:, gcols]
            st_new = _dot_tn(bg_, xd_b[:, gcols])
            state_scr[:, gcols] = st_prev * e_decay_x[L - 1:L, gcols] + st_new
            yg = y_diag + y_off + dexp[:, gcols] * xs[:, gcols]
            yg = yg * _silu(z_scr[rows_c, gcols])
            yg = yg * lax.rsqrt(jnp.mean(yg * yg, axis=-1, keepdims=True) + EPS)
            y_scr[rows_c, gcols] = (yg * nrm[:, gcols]).astype(BF16)

    def ssd_block(bi, carry):
        r0 = pl.multiple_of(bi * blk, blk)
        x = x_ref[pl.ds(r0, blk), :]
        hn = _norm_mod(x, ng, mod[0:1, :], mod[1:2, :]).astype(BF16)
        xpad_scr[halo:halo + blk, :] = _dot(hn, win_ref[:, din:din + cdim])
        dt_scr[...] = _dot(hn, win_ref[:, din + cdim:])
        conv_tiles = cdim // V7X_LANES
        z_tiles = din // Z_COLS
        for zi in range(z_tiles):
            zs = slice(zi * Z_COLS, (zi + 1) * Z_COLS)
            z_scr[:, zs] = _dot(hn, win_ref[:, zs])
            for ci in range(zi * conv_tiles // z_tiles, (zi + 1) * conv_tiles // z_tiles):
                cs = slice(ci * V7X_LANES, (ci + 1) * V7X_LANES)
                win = xpad_scr[:, cs]
                acc = cb[:, cs] + cw[kc - 1:kc, cs] * win[halo:halo + blk, :]
                for k in range(kc - 1):
                    acc = acc + cw[k:k + 1, cs] * pltpu.roll(win, kc - 1 - k, axis=0)[halo:halo + blk, :]
                act_scr[:, cs] = _silu(acc)
        xpad_scr[0:halo, :] = xpad_scr[blk:blk + halo, :]
        for c in range(blk // L):
            ssd_core(slice(c * L, (c + 1) * L))
        out = _dot(y_scr[...], wout_ref[...])
        o_ref[pl.ds(r0, blk), :] = x + mod[2:3, :] * out
        return carry

    lax.fori_loop(0, rows // blk, ssd_block, 0)


def _ssd_layer(x, mod, ng, win, cw, cb, dtb, alog, dexp, nrm, expand, wout, *, layer, mixer, rows=1024, blk=256):
    bsz, s, d = x.shape
    din = wout.shape[1]
    cdim = cw.shape[2]
    res = [win, cw, cb, dtb, alog, dexp, nrm]
    return pl.pallas_call(
        _ssd_body,
        out_shape=jax.ShapeDtypeStruct(x.shape, F32),
        grid=(bsz, s // rows),
        in_specs=[
            pl.BlockSpec((None, rows, d), lambda b, j: (b, j, 0)),
            pl.BlockSpec((None, None, 6, d), lambda b, j: (layer, b, 0, 0)),
            _resident(ng.shape, layer),
        ] + [_resident(a.shape, mixer) for a in res] + [_resident(expand.shape), _resident(wout.shape, mixer)],
        out_specs=pl.BlockSpec((None, rows, d), lambda b, j: (b, j, 0)),
        scratch_shapes=[
            pltpu.VMEM((blk, din), F32),
            pltpu.VMEM((V7X_SUBLANES + blk, cdim), F32),
            pltpu.VMEM((blk, cdim), F32),
            pltpu.VMEM((blk, dtb.shape[2]), F32),
            pltpu.VMEM((blk, din), BF16),
            pltpu.VMEM((SSD_STATE, din), F32),
        ],
        compiler_params=_params(("parallel", "arbitrary")),
        name="ssd_layer",
    )(x, mod, ng, *res, expand, wout)


def _route(logits_t, n_groups):
    ne, tm = logits_t.shape
    per = ne // n_groups
    ex = jnp.exp(logits_t - jnp.max(logits_t, axis=0, keepdims=True))
    probs = ex / jnp.sum(ex, axis=0, keepdims=True)
    p3 = probs.reshape(n_groups, per, tm)
    io = lax.broadcasted_iota(jnp.int32, p3.shape, 1)
    m1 = jnp.max(p3, axis=1, keepdims=True)
    sel1 = io == jnp.min(jnp.where(p3 == m1, io, per), axis=1, keepdims=True)
    rest = jnp.where(sel1, -1.0, p3)
    m2 = jnp.max(rest, axis=1, keepdims=True)
    sel2 = io == jnp.min(jnp.where(rest == m2, io, per), axis=1, keepdims=True)
    score = m1 + m2
    gio = lax.broadcasted_iota(jnp.int32, score.shape, 0)
    best = jnp.min(jnp.where(score == jnp.max(score, axis=0, keepdims=True), gio, n_groups),
                   axis=0, keepdims=True)
    top = jnp.where(sel1, m1, 0.0) + jnp.where(sel2, m2, 0.0)
    in_best = gio == best
    gates = jnp.sum(jnp.where(in_best, top / score, 0.0), axis=0)
    return gates, in_best.astype(F32).reshape(n_groups, tm)


def _moe_body(x_ref, mod_ref, ng_ref, rw2_ref, rwh_ref, rb_ref, wg_ref, wu_ref, wd_ref, fg_ref,
              o_ref, reg_scr, greg_scr, perm_scr, srt_scr, gsrt_scr, hi_scr, tab_smem, cnt_smem, *, final_norm):
    tm, d = x_ref.shape
    ngroups = N_EXPERT_GROUPS
    per = wg_ref.shape[0] // ngroups
    nsub = tm // MOE_SUB
    sp = perm_scr.shape[1]
    mod = mod_ref[...]

    first = (pl.program_id(0) == 0) & (pl.program_id(1) == 0)

    @pl.when(first)
    def _():
        reg_scr[...] = jnp.zeros_like(reg_scr)
        greg_scr[...] = jnp.zeros_like(greg_scr)
        srt_scr[...] = jnp.zeros_like(srt_scr)
        gsrt_scr[...] = jnp.zeros_like(gsrt_scr)

    def route_and_sort():
        for k in range(ngroups):
            cnt_smem[k] = 0
        tok_i = lax.broadcasted_iota(jnp.int32, (MOE_SUB, MOE_SUB), 0)
        tok_j = lax.broadcasted_iota(jnp.int32, (MOE_SUB, MOE_SUB), 1)
        triu = (tok_i <= tok_j).astype(BF16)
        row_i = lax.broadcasted_iota(jnp.int32, (sp, MOE_SUB), 0).astype(F32)
        hn = _norm_mod(x_ref[...], ng_ref[...], mod[3:4, :], mod[4:5, :])
        hi = hn.astype(BF16)
        lo = (hn - hi.astype(F32)).astype(BF16)
        hi_scr[...] = hi
        l2 = _dot(hi, rw2_ref[...])
        logits = l2[:, :V7X_LANES] + l2[:, V7X_LANES:] + _dot(lo, rwh_ref[...]) + rb_ref[...]
        gates4, member = _route(logits.T[0:ngroups * per, :], ngroups)
        g_hi = gates4.astype(BF16).astype(F32)
        g_cat = jnp.concatenate(
            [g_hi, gates4 - g_hi, jnp.zeros((V7X_LANES - 2 * per, tm), F32)], axis=0).astype(BF16)
        member_b = member.astype(BF16)
        subs = [slice(s * MOE_SUB, (s + 1) * MOE_SUB) for s in range(nsub)]
        cums = [_dot(member_b[:, sl], triu) for sl in subs]
        segments = []
        for s in range(nsub):
            cum = cums[s]
            cnt = cum[:, MOE_SUB - 1:MOE_SUB]
            c16 = jnp.floor((cnt + (ROW_ALIGN - 1.0)) * (1.0 / ROW_ALIGN)) * ROW_ALIGN
            bases = [jnp.zeros((1, 1), F32)]
            for k in range(1, ngroups):
                bases.append(bases[-1] + c16[k - 1:k, :])
            base = jnp.concatenate(bases, axis=0)
            pos = jnp.sum(member[:, subs[s]] * (cum - 1.0 + base), axis=0, keepdims=True)
            perm_scr[s] = jnp.where(row_i == pos, 1.0, 0.0).astype(BF16)
            segments.append((base, c16))
        for s in range(nsub):
            srt_scr[s, 0:sp, :] = _dot(perm_scr[s], hi_scr[subs[s], :]).astype(BF16)
            gsrt_scr[s, 0:sp, :] = _dot_nt(perm_scr[s], g_cat[:, subs[s]])
        for s in range(nsub):
            base, c16 = segments[s]
            for k in range(ngroups):
                b0 = pl.multiple_of(base[k, 0].astype(jnp.int32), ROW_ALIGN)
                r0 = pl.multiple_of(cnt_smem[k], ROW_ALIGN)
                t0 = (s * ngroups + k) * 2
                tab_smem[t0] = b0
                tab_smem[t0 + 1] = r0
                reg_scr[k, pl.ds(r0, MOE_SUB), :] = srt_scr[s, pl.ds(b0, MOE_SUB), :]
                greg_scr[k, pl.ds(r0, MOE_SUB), :] = gsrt_scr[s, pl.ds(b0, MOE_SUB), :]
                cnt_smem[k] = r0 + c16[k, 0].astype(jnp.int32)

    def group_step(grp, carry):
        def experts(off, m):
            h = reg_scr[grp, pl.ds(off, m), :]
            gt = greg_scr[grp, pl.ds(off, m), :]
            acts = []
            for e in range(per):
                gate = gt[:, e:e + 1] + gt[:, per + e:per + e + 1]
                a = _silu(_dot(h, wg_ref[grp * per + e])) * _dot(h, wu_ref[grp * per + e]) * gate
                acts.append(a.astype(BF16))
            out = _dot(jnp.concatenate(acts, axis=1), wd_ref[grp])
            reg_scr[grp, pl.ds(off, m), :] = out.astype(BF16)

        n_rows = cnt_smem[grp]
        even = [tm // ngroups + (k + 1) * ROW_ALIGN for k in range(nsub)]
        below = 0
        for m in even:
            @pl.when((n_rows > below) & (n_rows <= m))
            def _(m=m):
                experts(0, m)
            below = m

        @pl.when(n_rows > even[-1])
        def _():
            n_full = n_rows // MOE_CHUNK

            def full_chunk(ci, c2):
                experts(pl.multiple_of(ci * MOE_CHUNK, MOE_CHUNK), MOE_CHUNK)
                return c2

            lax.fori_loop(0, n_full, full_chunk, 0)
            rem = n_rows - n_full * MOE_CHUNK
            tail = pl.multiple_of(n_full * MOE_CHUNK, MOE_CHUNK)

            @pl.when(rem > MOE_CHUNK // 2)
            def _():
                experts(tail, MOE_CHUNK)

            @pl.when((rem > 0) & (rem <= MOE_CHUNK // 2))
            def _():
                experts(tail, MOE_CHUNK // 2)
        return carry

    route_and_sort()
    lax.fori_loop(0, ngroups, group_step, 0)
    if True:
        for s in range(nsub):
            rows = slice(s * MOE_SUB, (s + 1) * MOE_SUB)
            for k in range(ngroups):
                t0 = (s * ngroups + k) * 2
                b0 = pl.multiple_of(tab_smem[t0], ROW_ALIGN)
                r0 = pl.multiple_of(tab_smem[t0 + 1], ROW_ALIGN)
                srt_scr[s, pl.ds(b0, MOE_SUB), :] = reg_scr[k, pl.ds(r0, MOE_SUB), :]
            out = _dot_tn(perm_scr[s], srt_scr[s, 0:sp, :])
            y = x_ref[rows, :] + mod[5:6, :] * out
            if final_norm:
                y = y * lax.rsqrt(jnp.mean(y * y, axis=-1, keepdims=True) + EPS) * fg_ref[...]
            o_ref[rows, :] = y


def _moe_layer(x, mod, ng, rw2, rwh, rb, wg, wu, wd, fg, *, layer, final_norm, tm=512):
    bsz, s, d = x.shape
    _, ne, _, de = wg.shape
    ngroups = N_EXPERT_GROUPS
    per = ne // ngroups
    nsub = tm // MOE_SUB
    sp = MOE_SUB + ngroups * ROW_ALIGN
    cap = tm + nsub * ngroups * ROW_ALIGN + max(MOE_CHUNK, MOE_SUB)
    small = [rw2, rwh, rb]
    return pl.pallas_call(
        functools.partial(_moe_body, final_norm=final_norm),
        out_shape=jax.ShapeDtypeStruct(x.shape, F32),
        grid=(bsz, s // tm),
        in_specs=[
            pl.BlockSpec((None, tm, d), lambda b, j: (b, j, 0)),
            pl.BlockSpec((None, None, 6, d), lambda b, j: (layer, b, 0, 0)),
            _resident(ng.shape, layer),
        ] + [_resident(a.shape) for a in small] + [
            _resident(wg.shape, layer), _resident(wu.shape, layer),
            _resident((wd.shape[0], ngroups, per * de, d), layer),
            _resident(fg.shape),
        ],
        out_specs=pl.BlockSpec((None, tm, d), lambda b, j: (b, j, 0)),
        scratch_shapes=[
            pltpu.VMEM((ngroups, cap, d), BF16),
            pltpu.VMEM((ngroups, cap, V7X_LANES), F32),
            pltpu.VMEM((nsub, sp, MOE_SUB), BF16),
            pltpu.VMEM((nsub, sp + MOE_SUB, d), BF16),
            pltpu.VMEM((nsub, sp + MOE_SUB, V7X_LANES), F32),
            pltpu.VMEM((tm, d), BF16),
            pltpu.SMEM((nsub * ngroups * 2,), jnp.int32),
            pltpu.SMEM((ngroups,), jnp.int32),
        ],
        compiler_params=_params(("arbitrary", "arbitrary")),
        name="moe_layer",
    )(x, mod, ng, *small, wg, wu, wd.reshape(wd.shape[0], ngroups, per * de, d), fg)


def kernel(x, c, ada_w, ada_b, norm1_g, norm2_g, lru_w_in, lru_conv_w, lru_conv_b, lru_wa, lru_ba, lru_wx, lru_bx, lru_lambda, lru_w_out, ssd_w_in, ssd_conv_w, ssd_conv_b, ssd_dt_bias, ssd_a_log, ssd_d, ssd_norm_g, ssd_w_out, router_w, router_b, moe_w_gate, moe_w_up, moe_w_down, final_norm_g):
    bsz, s, d = x.shape
    depth = ada_w.shape[0]
    n_mixers = 2
    mod = _ada_mod(c, ada_w, ada_b).reshape(depth, bsz, 6, d)
    ng1 = norm1_g.reshape(depth, 1, d)
    ng2 = norm2_g.reshape(depth, 1, d)

    n_lru, nh = lru_wa.shape[:2]
    w = lru_w_out.shape[1]
    lru_args = (
        lru_w_in.astype(BF16), lru_conv_w, lru_conv_b.reshape(n_lru, 1, w),
        (0.5 * jnp.concatenate([lru_wa, lru_wx], axis=-1)).astype(BF16),
        0.5 * jnp.concatenate([lru_ba.reshape(n_lru, nh, 1, w // nh), lru_bx.reshape(n_lru, nh, 1, w // nh)], axis=-1),
        lru_lambda.reshape(n_lru, 1, w), lru_w_out.astype(BF16))

    n_ssd, heads = ssd_dt_bias.shape
    din = heads * SSD_HEAD_DIM
    cdim = din + 2 * SSD_GROUPS * SSD_STATE
    expand = (jnp.arange(heads)[:, None] == (jnp.arange(din) // SSD_HEAD_DIM)[None, :]).astype(BF16)
    ssd_args = (
        ssd_w_in.astype(BF16), ssd_conv_w, ssd_conv_b.reshape(n_ssd, 1, cdim),
        ssd_dt_bias.reshape(n_ssd, 1, heads), ssd_a_log.reshape(n_ssd, 1, heads),
        jnp.repeat(ssd_d, SSD_HEAD_DIM, axis=1).reshape(n_ssd, 1, din),
        ssd_norm_g.reshape(n_ssd, 1, din), expand, ssd_w_out.astype(BF16))
    moe_args = (moe_w_gate.astype(BF16), moe_w_up.astype(BF16), moe_w_down.astype(BF16))

    ne = router_w.shape[1]
    rw = jnp.pad(router_w, ((0, 0), (0, V7X_LANES - ne)))
    rwh = rw.astype(BF16)
    rw2 = jnp.concatenate([rwh, (rw - rwh.astype(F32)).astype(BF16)], axis=1)
    rb = jnp.pad(router_b, (0, V7X_LANES - ne)).reshape(1, V7X_LANES)
    fg = final_norm_g.reshape(1, d)

    for i in range(depth):
        j = i // n_mixers
        if i % n_mixers == 0:
            x = _lru_layer(x, mod, ng1, *lru_args, layer=i, mixer=j)
        else:
            x = _ssd_layer(x, mod, ng1, *ssd_args, layer=i, mixer=j)
        x = _moe_layer(x, mod, ng2, rw2, rwh, rb, *moe_args, fg, layer=i, final_norm=(i == depth - 1))
    return x
```
